```python
import jax, jax.numpy as jnp
from jax import lax
import numpy as np

D_MODEL = 1024
BATCH = 2
SEQ = 8192
DEPTH = 4
DEC_BATCH = 32
DEC_SEQ = 8
PAST_LEN = 8192
PAGE_SIZE = 128

HEAD_DIM = 64
ATT_HEADS = 8
ATT_GROUPS = ((128, 1), (512, 4), (2048, 16))
N_GROUPS = len(ATT_GROUPS)
ATT_WIDTH = ATT_HEADS * HEAD_DIM
BLK = 128
ROPE_THETA = 10000.0
RW_HEADS = 8
RW_WIDTH = RW_HEADS * HEAD_DIM
D_DECAY_LORA = 64
D_AAA_LORA = 64
D_GATE_LORA = 160
D_MV_LORA = 32
D_FF = 2816
CONV_W = 3
LN_EPS = 1e-5
GN_EPS = 64e-5
ALPHA = (2.0 * DEPTH) ** 0.25
BETA = (8.0 * DEPTH) ** -0.25
C_ATT = 3 * N_GROUPS * ATT_WIDTH
C_GATE = 2 * D_MODEL
C_RW = 3 * RW_WIDTH + D_DECAY_LORA + D_AAA_LORA + D_GATE_LORA
C_MAIN = C_ATT + C_GATE + C_RW

kernel_name = 'hybrid_rwkv7_dilated_swa_convffn_step'


def layer_norm(x, g, b):
    xf = x.astype(jnp.float32)
    mu = jnp.mean(xf, axis=-1, keepdims=True)
    var = jnp.mean(jnp.square(xf - mu), axis=-1, keepdims=True)
    return ((xf - mu) * lax.rsqrt(var + LN_EPS)).astype(x.dtype) * g + b


def rope(t, pos):
    half = HEAD_DIM // 2
    inv = ROPE_THETA ** (-jnp.arange(half, dtype=jnp.float32) / half)
    ang = pos.astype(jnp.float32)[:, None] * inv[None, :]
    cos = jnp.cos(ang)[None, :, None, :].astype(t.dtype)
    sin = jnp.sin(ang)[None, :, None, :].astype(t.dtype)
    t1, t2 = t[..., :half], t[..., half:]
    return jnp.concatenate([t1 * cos - t2 * sin, t1 * sin + t2 * cos], axis=-1)


def masked_softmax_lse(s, mask):
    s = jnp.where(mask, s, -jnp.inf)
    m = jnp.max(s, axis=-1, keepdims=True)
    p = jnp.exp(s - m)
    l = jnp.sum(p, axis=-1, keepdims=True)
    return p / l, (m + jnp.log(l))[..., 0]


def dilated_attn_prompt(q, k, v, window, dil):
    B, S, H, E = q.shape
    reach = window // dil
    nprev = -(-reach // BLK)
    L = -(-S // dil)
    nb = -(-L // BLK)
    Lp = nb * BLK
    pad = Lp * dil - S

    def split(t):
        t = jnp.pad(t, ((0, 0), (0, pad), (0, 0), (0, 0)))
        return t.reshape(B, Lp, dil, H, E).transpose(0, 2, 1, 3, 4).reshape(B, dil, nb, BLK, H, E)

    def band(t):
        tp = jnp.pad(t, ((0, 0), (0, 0), (nprev, 0), (0, 0), (0, 0), (0, 0)))
        return jnp.concatenate([tp[:, :, j:j + nb] for j in range(nprev + 1)], axis=3)

    qb = split(q)
    kb, vb = band(split(k)), band(split(v))
    s = jnp.einsum('brnqhe,brnkhe->brnhqk', qb, kb, preferred_element_type=jnp.float32) * (E ** -0.5)
    qi = jnp.arange(BLK)[:, None]
    kj = jnp.arange((nprev + 1) * BLK)[None, :]
    dist = nprev * BLK + qi - kj
    key_idx = (jnp.arange(nb)[:, None, None] - nprev) * BLK + kj[None]
    mask = (dist >= 0) & (dist <= reach) & (key_idx >= 0)
    prob, lse = masked_softmax_lse(s, mask[None, None, :, None])
    o = jnp.einsum('brnhqk,brnkhe->brnqhe', prob.astype(v.dtype), vb)
    o = o.reshape(B, dil, Lp, H, E).transpose(0, 2, 1, 3, 4).reshape(B, Lp * dil, H, E)[:, :S]
    lse = lse.transpose(0, 1, 2, 4, 3).reshape(B, dil, Lp, H).transpose(0, 2, 1, 3).reshape(B, Lp * dil, H)[:, :S]
    return o, lse


def dilated_attn_sample(q, k, v, k_buf, v_buf, window, dil):
    T, E = q.shape[1], q.shape[-1]
    wb = k_buf.shape[1]
    kc = jnp.concatenate([k_buf.astype(k.dtype), k], axis=1)
    vc = jnp.concatenate([v_buf.astype(v.dtype), v], axis=1)
    nk = window // dil + 1
    idx = wb + jnp.arange(T)[:, None] - dil * jnp.arange(nk)[None, :]
    valid = idx >= 0
    idx = jnp.maximum(idx, 0)
    kg, vg = kc[:, idx], vc[:, idx]
    s = jnp.einsum('bthe,btjhe->bthj', q, kg, preferred_element_type=jnp.float32) * (E ** -0.5)
    prob, lse = masked_softmax_lse(s, valid[None, :, None, :])
    o = jnp.einsum('bthj,btjhe->bthe', prob.astype(v.dtype), vg)
    new_buf = jnp.stack([kc[:, T:], vc[:, T:]], axis=1)
    return o, lse, new_buf


def wkv_scan(r, decay, k, v, kk, a, s0):
    f32 = jnp.float32
    xs = tuple(jnp.moveaxis(t.astype(f32), 1, 0) for t in (r, decay, k, v, kk, a))

    def step(s, inp):
        r_t, w_t, k_t, v_t, kk_t, a_t = inp
        sa = jnp.einsum('bhvk,bhk->bhv', s, -kk_t)
        s = s * w_t[:, :, None, :] + sa[..., None] * (kk_t * a_t)[:, :, None, :] + v_t[..., None] * k_t[:, :, None, :]
        return s, jnp.einsum('bhvk,bhk->bhv', s, r_t)

    s, ys = lax.scan(step, s0.astype(f32), xs)
    return jnp.moveaxis(ys, 0, 1), s


def rwkv_time_mix(p_rw, p_prev, wkv0, v_first, p):
    B, T, _ = p_rw.shape
    z = p_rw + (p_prev - p_rw) * p['mu']
    c = RW_WIDTH
    r, k, v = z[..., :c], z[..., c:2 * c], z[..., 2 * c:3 * c]
    o = 3 * c
    wl = z[..., o:o + D_DECAY_LORA]
    o += D_DECAY_LORA
    al = z[..., o:o + D_AAA_LORA]
    o += D_AAA_LORA
    gl = z[..., o:o + D_GATE_LORA]
    o += D_GATE_LORA
    w_log = -jax.nn.softplus(-(p['w0'] + jnp.tanh(wl) @ p['w2'])) - 0.5
    decay = jnp.exp(-jnp.exp(w_log.astype(jnp.float32)))
    a = jax.nn.sigmoid(p['a0'] + al @ p['a2'])
    g = jax.nn.sigmoid(gl) @ p['g2']
    if v_first is None:
        v_first = v
    else:
        vl = z[..., o:o + D_MV_LORA]
        v = v + (v_first - v) * jax.nn.sigmoid(p['v0'] + vl @ p['v2'])

    def heads(t):
        return t.reshape(B, T, RW_HEADS, HEAD_DIM)

    kk = heads(k * p['kk']).astype(jnp.float32)
    kk = kk / jnp.maximum(jnp.sqrt(jnp.sum(kk * kk, axis=-1, keepdims=True)), 1e-12)
    k = k * (1.0 + (a - 1.0) * p['ka'])
    rh, kh, vh = heads(r), heads(k), heads(v)
    y, wkv = wkv_scan(rh, heads(decay), kh, vh, kk, heads(a), wkv0)
    mu = jnp.mean(y, axis=-1, keepdims=True)
    var = jnp.mean(jnp.square(y - mu), axis=-1, keepdims=True)
    y = ((y - mu) * lax.rsqrt(var + GN_EPS)).reshape(B, T, c).astype(p_rw.dtype) * p['gn_g'] + p['gn_b']
    bonus = jnp.sum(rh * kh * p['rk'], axis=-1, keepdims=True) * vh
    y = (y + bonus.reshape(B, T, c)) * g
    return y, wkv, v_first


def conv_ffn(x, conv_buf, w_up, conv_w, conv_b, w_down):
    T = x.shape[1]
    h = x @ w_up
    u, gt = h[..., :D_FF], h[..., D_FF:]
    ue = jnp.concatenate([conv_buf.astype(u.dtype), u], axis=1)
    c = conv_b + sum(conv_w[j] * ue[:, j:j + T] for j in range(CONV_W))
    y = (jax.nn.gelu(c, approximate=False) * gt) @ w_down
    return y, ue[:, T:]


def trunk_layer(x, pos, x_prev, wkv0, win_bufs, conv_buf, v_first, p):
    B, T, _ = x.shape
    w_in = p['w_in']
    P = x @ w_in
    outs, lses, new_win = [], [], []
    for g, (window, dil) in enumerate(ATT_GROUPS):
        def cols(blk):
            lo = (blk * N_GROUPS + g) * ATT_WIDTH
            return P[..., lo:lo + ATT_WIDTH].reshape(B, T, ATT_HEADS, HEAD_DIM)
        q, k, v = rope(cols(0), pos), rope(cols(1), pos), cols(2)
        if win_bufs is None:
            o, lse = dilated_attn_prompt(q, k, v, window, dil)
            wp = min(window, T)
            new_win.append(jnp.stack([k[:, T - wp:], v[:, T - wp:]], axis=1))
        else:
            o, lse, buf = dilated_attn_sample(q, k, v, win_bufs[g][0], win_bufs[g][1], window, dil)
            new_win.append(buf)
        outs.append(o)
        lses.append(lse)
    mix_w = jax.nn.softmax(jnp.stack(lses, axis=0), axis=0).astype(x.dtype)
    att = jnp.einsum('gbth,gbthe->bthe', mix_w, jnp.stack(outs, axis=0)).reshape(B, T, ATT_WIDTH)
    gate_a = jax.nn.sigmoid(P[..., C_ATT:C_ATT + D_MODEL])
    gate_b = jax.nn.sigmoid(P[..., C_ATT + D_MODEL:C_ATT + C_GATE])
    p_rw = P[..., C_ATT + C_GATE:]
    prev0 = x_prev @ w_in[:, C_ATT + C_GATE:]
    p_prev = jnp.concatenate([prev0[:, None, :], p_rw[:, :-1]], axis=1)
    rw, wkv_new, v_first = rwkv_time_mix(p_rw, p_prev, wkv0, v_first, p)
    merged = gate_a * (rw @ p['w_br_a']) + gate_b * (att @ p['w_br_b'])
    h = layer_norm(ALPHA * x + merged @ p['w_out'], p['ln1_g'], p['ln1_b'])
    f, conv_new = conv_ffn(h, conv_buf, p['ffn_w_up'], p['ffn_conv_w'], p['ffn_conv_b'], p['ffn_w_down'])
    y = layer_norm(ALPHA * h + f, p['ln2_g'], p['ln2_b'])
    return y, new_win, wkv_new, x[:, -1], conv_new, v_first


def setup_inputs(seed: int = 0) -> dict:
    key = jax.random.key(seed)
    ks = iter(jax.random.split(key, 64))

    def nrm(shape, scale):
        return jax.random.normal(next(ks), shape, jnp.float32) * scale

    def unif(shape):
        return jax.random.uniform(next(ks), shape, jnp.float32)

    L, Lv = DEPTH, DEPTH - 1
    wb = [min(w, PAST_LEN) for w, _ in ATT_GROUPS]
    return {
        'x_prompt': nrm((BATCH, SEQ, D_MODEL), 1.0),
        'x_sample': nrm((DEC_BATCH, DEC_SEQ, D_MODEL), 1.0),
        'cache_win128': nrm((L, DEC_BATCH, 2, wb[0], ATT_HEADS, HEAD_DIM), 1.0),
        'cache_win512': nrm((L, DEC_BATCH, 2, wb[1], ATT_HEADS, HEAD_DIM), 1.0),
        'cache_win2048': nrm((L, DEC_BATCH, 2, wb[2], ATT_HEADS, HEAD_DIM), 1.0),
        'state_wkv': nrm((L, DEC_BATCH, RW_HEADS, HEAD_DIM, HEAD_DIM), 0.1),
        'state_shift': nrm((L, DEC_BATCH, D_MODEL), 1.0),
        'state_ffn_conv': nrm((L, DEC_BATCH, CONV_W - 1, D_FF), 1.0),
        'w_in': nrm((L, D_MODEL, C_MAIN), D_MODEL ** -0.5),
        'w_in_vres': nrm((Lv, D_MODEL, D_MV_LORA), D_MODEL ** -0.5),
        'mu_rw': unif((L, C_RW)),
        'mu_vres': unif((Lv, D_MV_LORA)),
        'rw_w0': jnp.linspace(-6.5, -1.5, RW_WIDTH, dtype=jnp.float32)[None, :] + nrm((L, RW_WIDTH), 0.1),
        'rw_w2': nrm((L, D_DECAY_LORA, RW_WIDTH), 0.3 * D_DECAY_LORA ** -0.5),
        'rw_a0': nrm((L, RW_WIDTH), 0.1),
        'rw_a2': nrm((L, D_AAA_LORA, RW_WIDTH), 0.5 * D_AAA_LORA ** -0.5),
        'rw_g2': nrm((L, D_GATE_LORA, RW_WIDTH), D_GATE_LORA ** -0.5),
        'rw_v0': nrm((Lv, RW_WIDTH), 0.1),
        'rw_v2': nrm((Lv, D_MV_LORA, RW_WIDTH), 0.5 * D_MV_LORA ** -0.5),
        'rw_kk': 0.85 + nrm((L, RW_WIDTH), 0.05),
        'rw_ka': 1.0 + nrm((L, RW_WIDTH), 0.05),
        'rw_rk': nrm((L, RW_HEADS, HEAD_DIM), 0.1),
        'rw_gn_g': 1.0 + nrm((L, RW_WIDTH), 0.05),
        'rw_gn_b': nrm((L, RW_WIDTH), 0.02),
        'w_br_a': nrm((L, RW_WIDTH, D_MODEL), RW_WIDTH ** -0.5),
        'w_br_b': nrm((L, ATT_WIDTH, D_MODEL), ATT_WIDTH ** -0.5),
        'w_out': nrm((L, D_MODEL, D_MODEL), BETA * D_MODEL ** -0.5),
        'ln1_g': 1.0 + nrm((L, D_MODEL), 0.05),
        'ln1_b': nrm((L, D_MODEL), 0.02),
        'ffn_w_up': nrm((L, D_MODEL, 2 * D_FF), D_MODEL ** -0.5),
        'ffn_conv_w': nrm((L, CONV_W, D_FF), 0.5),
        'ffn_conv_b': nrm((L, D_FF), 0.02),
        'ffn_w_down': nrm((L, D_FF, D_MODEL), BETA * D_FF ** -0.5),
        'ln2_g': 1.0 + nrm((L, D_MODEL), 0.05),
        'ln2_b': nrm((L, D_MODEL), 0.02),
    }


def reference(x_prompt, x_sample, cache_win128, cache_win512, cache_win2048, state_wkv, state_shift, state_ffn_conv,
              w_in, w_in_vres, mu_rw, mu_vres, rw_w0, rw_w2, rw_a0, rw_a2, rw_g2, rw_v0, rw_v2, rw_kk, rw_ka, rw_rk,
              rw_gn_g, rw_gn_b, w_br_a, w_br_b, w_out, ln1_g, ln1_b, ffn_w_up, ffn_conv_w, ffn_conv_b, ffn_w_down,
              ln2_g, ln2_b):
    caches = (cache_win128, cache_win512, cache_win2048)
    B, S, _ = x_prompt.shape
    pos_p = jnp.arange(S, dtype=jnp.int32)
    pos_s = PAST_LEN + jnp.arange(x_sample.shape[1], dtype=jnp.int32)
    xp, xs = x_prompt, x_sample
    vf_p = None
    vf_s = None
    win_p = [[] for _ in ATT_GROUPS]
    win_s = [[] for _ in ATT_GROUPS]
    wkv_p, wkv_s, sh_p, sh_s, cv_p, cv_s = [], [], [], [], [], []
    for l in range(DEPTH):
        if l == 0:
            w_l, mu_l, v0_l, v2_l = w_in[0], mu_rw[0], None, None
        else:
            w_l = jnp.concatenate([w_in[l], w_in_vres[l - 1]], axis=1)
            mu_l = jnp.concatenate([mu_rw[l], mu_vres[l - 1]], axis=0)
            v0_l, v2_l = rw_v0[l - 1], rw_v2[l - 1]
        p = {'w_in': w_l, 'mu': mu_l, 'w0': rw_w0[l], 'w2': rw_w2[l], 'a0': rw_a0[l], 'a2': rw_a2[l],
             'g2': rw_g2[l], 'v0': v0_l, 'v2': v2_l, 'kk': rw_kk[l], 'ka': rw_ka[l], 'rk': rw_rk[l],
             'gn_g': rw_gn_g[l], 'gn_b': rw_gn_b[l], 'w_br_a': w_br_a[l], 'w_br_b': w_br_b[l], 'w_out': w_out[l],
             'ln1_g': ln1_g[l], 'ln1_b': ln1_b[l], 'ffn_w_up': ffn_w_up[l], 'ffn_conv_w': ffn_conv_w[l],
             'ffn_conv_b': ffn_conv_b[l], 'ffn_w_down': ffn_w_down[l], 'ln2_g': ln2_g[l], 'ln2_b': ln2_b[l]}
        xp, nw, s_new, sh, cv, vf_p = trunk_layer(
            xp, pos_p, jnp.zeros((B, D_MODEL), xp.dtype),
            jnp.zeros((B, RW_HEADS, HEAD_DIM, HEAD_DIM), jnp.float32), None,
            jnp.zeros((B, CONV_W - 1, D_FF), xp.dtype), vf_p, p)
        for g in range(N_GROUPS):
            win_p[g].append(nw[g])
        wkv_p.append(s_new)
        sh_p.append(sh)
        cv_p.append(cv)
        bufs = [(c[l, :, 0], c[l, :, 1]) for c in caches]
        xs, nw, s_new, sh, cv, vf_s = trunk_layer(
            xs, pos_s, state_shift[l], state_wkv[l], bufs, state_ffn_conv[l], vf_s, p)
        for g in range(N_GROUPS):
            win_s[g].append(nw[g])
        wkv_s.append(s_new)
        sh_s.append(sh)
        cv_s.append(cv)
    return (xp, xs, jnp.stack(win_p[0]), jnp.stack(win_s[0]), jnp.stack(win_p[1]), jnp.stack(win_s[1]),
            jnp.stack(win_p[2]), jnp.stack(win_s[2]), jnp.stack(wkv_p), jnp.stack(wkv_s), jnp.stack(sh_p),
            jnp.stack(sh_s), jnp.stack(cv_p), jnp.stack(cv_s))
```

```python
import functools

import jax
import jax.numpy as jnp
from jax import lax
from jax.experimental import pallas as pl
from jax.experimental.pallas import tpu as pltpu

F32 = jnp.float32
BF16 = jnp.bfloat16

D_MODEL = 1024
HEAD_DIM = 64
N_HEADS = 8
WIDTH = N_HEADS * HEAD_DIM
ATT_GROUPS = ((128, 1), (512, 4), (2048, 16))
N_GROUPS = len(ATT_GROUPS)
BLK = 128
ROPE_THETA = 10000.0
D_DECAY_LORA = 64
D_AAA_LORA = 64
D_GATE_LORA = 160
D_MV_LORA = 32
D_FF = 2816
CONV_W = 3
LN_EPS = 1e-5
GN_EPS = 64e-5
DEPTH = 4
ALPHA = (2.0 * DEPTH) ** 0.25
PAST_LEN = 8192
C_ATT = 3 * N_GROUPS * WIDTH
C_GATE = 2 * D_MODEL
C_RW = 3 * WIDTH + D_DECAY_LORA + D_AAA_LORA + D_GATE_LORA
C_RW_PAD = 1920
LORA_WA = 3 * WIDTH
LORA_GV = LORA_WA + 128
FF_CHUNK = 256
WKV_CHUNK = 64
LANE = 128
VMEM_LIMIT = 56 * 1024 * 1024
NEG_INF = float("-inf")


def _params(n_axes):
    return pltpu.CompilerParams(
        dimension_semantics=("arbitrary",) * n_axes, vmem_limit_bytes=VMEM_LIMIT)


def _dot(a, b):
    return jnp.dot(a, b, preferred_element_type=F32)


def _dot_nt(a, b, precision=None):
    return lax.dot_general(a, b, (((1,), (1,)), ((), ())), precision=precision,
                           preferred_element_type=F32)


def _dot_tn(a, b, precision=None):
    return lax.dot_general(a, b, (((0,), (0,)), ((), ())), precision=precision,
                           preferred_element_type=F32)


def _dot_hi(a, b):
    return jnp.dot(a, b, precision=lax.Precision.HIGHEST, preferred_element_type=F32)


def _head_sum(x, ones_bd):
    hi = x.astype(BF16)
    lo = (x - hi.astype(F32)).astype(BF16)
    return _dot(hi, ones_bd) + _dot(lo, ones_bd)


def _layer_norm(x, g, b):
    mu = jnp.mean(x, axis=-1, keepdims=True)
    d = x - mu
    var = jnp.mean(d * d, axis=-1, keepdims=True)
    return d * lax.rsqrt(var + LN_EPS) * g + b


def _sigmoid(x):
    return 1.0 / (1.0 + jnp.exp(-x))


def _rope(t, cos, sin):
    half = HEAD_DIM // 2
    outs = []
    for c in range(t.shape[-1] // LANE):
        tc = t[:, c * LANE:(c + 1) * LANE]
        lane = lax.broadcasted_iota(jnp.int32, tc.shape, 1)
        fwd = pltpu.roll(tc, LANE - half, 1)
        bwd = pltpu.roll(tc, half, 1)
        outs.append(jnp.where((lane & (HEAD_DIM - 1)) < half, fwd, bwd))
    partner = jnp.concatenate(outs, axis=-1)
    return t * cos + partner * sin


def _mm_body(x_ref, w_ref, o_ref, xb_ref):
    @pl.when(pl.program_id(1) == 0)
    def _():
        xb_ref[...] = x_ref[...].astype(BF16)

    o_ref[...] = _dot(xb_ref[...], w_ref[...])


def _mm(x, w, tm, tn):
    m, k = x.shape
    n = w.shape[1]
    return pl.pallas_call(
        _mm_body,
        grid=(m // tm, n // tn),
        in_specs=[pl.BlockSpec((tm, k), lambda i, j: (i, 0)),
                  pl.BlockSpec((k, tn), lambda i, j: (0, j))],
        out_specs=pl.BlockSpec((tm, tn), lambda i, j: (i, j)),
        out_shape=jax.ShapeDtypeStruct((m, n), F32),
        scratch_shapes=[pltpu.VMEM((tm, k), BF16)],
        compiler_params=_params(2),
        name="proj_mm",
    )(x, w)


def _tm_body(*refs, seq, tm, carry_mode):
    if carry_mode:
        (p_ref, prev0_ref, mu_ref, w0_ref, a0_ref, v0_ref, kkp_ref, kap_ref, w2a2_ref, g2v2_ref,
         ones_ref, vf_ref, l0_ref,
         r_out, lw_out, k_out, v_out, kk_out, a_out, g_out, carry_ref) = refs
    else:
        (p_ref, prev0_ref, mu_ref, w0_ref, a0_ref, v0_ref, kkp_ref, kap_ref, w2a2_ref, g2v2_ref,
         ones_ref, vf_ref, l0_ref,
         r_out, lw_out, k_out, v_out, kk_out, a_out, g_out) = refs
    p = p_ref[...]
    row = lax.broadcasted_iota(jnp.int32, p.shape, 0)
    rolled = pltpu.roll(p, 1, 0)
    if carry_mode:
        slot = pl.program_id(0) % 2

        @pl.when(pl.program_id(0) % (seq // tm) == 0)
        def _():
            carry_ref[1 - slot, 0:1, :] = prev0_ref[0]

        p_prev = jnp.where(row == 0, carry_ref[1 - slot, 0:1, :], rolled)
        carry_ref[slot, 0:1, :] = p[tm - 1:tm, :]
    else:
        p_prev = jnp.where(row % seq == 0, prev0_ref[...], rolled)
    z = p + (p_prev - p) * mu_ref[...]
    r = z[:, :WIDTH]
    k = z[:, WIDTH:2 * WIDTH]
    v = z[:, 2 * WIDTH:3 * WIDTH]
    wa = z[:, LORA_WA:LORA_WA + 128]
    lane = lax.broadcasted_iota(jnp.int32, wa.shape, 1)
    wa = jnp.where(lane < D_DECAY_LORA, jnp.tanh(wa), wa)
    wa_out = _dot(wa.astype(BF16), w2a2_ref[...])
    gv = z[:, LORA_GV:LORA_GV + 256]
    lane = lax.broadcasted_iota(jnp.int32, gv.shape, 1)
    gv = jnp.where(lane < D_GATE_LORA, _sigmoid(gv), gv)
    gv_out = _dot(gv.astype(BF16), g2v2_ref[...])
    x = -(w0_ref[...] + wa_out[:, :WIDTH])
    softplus = jnp.maximum(x, 0.0) + jnp.log1p(jnp.exp(-jnp.abs(x)))
    lw_out[...] = -jnp.exp(-softplus - 0.5)
    a = _sigmoid(a0_ref[...] + wa_out[:, WIDTH:])
    g_out[...] = gv_out[:, :WIDTH]
    vf = jnp.where(l0_ref[...] > 0.5, v, vf_ref[...])
    v = v + (vf - v) * _sigmoid(v0_ref[...] + gv_out[:, WIDTH:])
    kk = k * kkp_ref[...]
    norm = jnp.sqrt(_head_sum(kk * kk, ones_ref[...]))
    kk_out[...] = kk / jnp.maximum(norm, 1e-12)
    k_out[...] = k * (1.0 + (a - 1.0) * kap_ref[...])
    r_out[...] = r
    v_out[...] = v
    a_out[...] = a


def _time_mix(p_rw, prev0, vfirst, lw, seq, tm, carry_mode):
    m = p_rw.shape[0]
    row = lambda c: pl.BlockSpec((tm, c), lambda i: (i, 0))
    const = lambda a: pl.BlockSpec(a.shape, lambda i: (0,) * a.ndim)
    if carry_mode:
        prev_spec = pl.BlockSpec((1, 1, C_RW_PAD), lambda i: (i // (seq // tm), 0, 0))
        scratch = [pltpu.VMEM((2, 8, C_RW_PAD), F32)]
    else:
        prev_spec = row(C_RW_PAD)
        scratch = []
    consts = [lw["mu"], lw["w0"], lw["a0"], lw["v0"], lw["kk"], lw["ka"], lw["w2a2"], lw["g2v2"],
              lw["ones_bd"]]
    out = jax.ShapeDtypeStruct((m, WIDTH), F32)
    return pl.pallas_call(
        functools.partial(_tm_body, seq=seq, tm=tm, carry_mode=carry_mode),
        grid=(m // tm,),
        in_specs=[row(C_RW_PAD), prev_spec] + [const(a) for a in consts]
        + [row(WIDTH), const(lw["is_l0"])],
        out_specs=[row(WIDTH)] * 7,
        out_shape=[out] * 7,
        scratch_shapes=scratch,
        compiler_params=_params(1),
        name="time_mix",
    )(p_rw, prev0, *consts, vfirst, lw["is_l0"])


def _wkv_body(r_ref, lw_ref, k_ref, v_ref, kk_ref, a_ref, s0_ref, y_ref, sout_ref, s_ref, *, nc):
    c = pl.program_id(1)
    n = WKV_CHUNK

    @pl.when(c == 0)
    def _():
        s_ref[...] = s0_ref[0]

    lw = lw_ref[0]
    row = lax.broadcasted_iota(jnp.int32, (n, n), 0)
    col = lax.broadcasted_iota(jnp.int32, (n, n), 1)
    incl = col <= row
    strict = col < row
    eye = (col == row).astype(F32)
    cum = _dot_hi(incl.astype(F32), lw)
    cum_end = cum[n - 1:n, :]
    kk = kk_ref[0]
    k = k_ref[0]
    kka = kk * a_ref[0]
    e_neg = jnp.exp(-cum)
    e_end = jnp.exp(cum_end - cum)
    a_t = -kk * jnp.exp(cum - lw)
    b_t = kka * e_neg
    k_t = k * e_neg
    r_t = r_ref[0] * jnp.exp(cum)
    b_h = kka * e_end
    k_h = k * e_end
    g_end = jnp.exp(cum_end)
    v = v_ref[0]
    hi = lax.Precision.HIGHEST
    for h in range(N_HEADS):
        sl = slice(h * HEAD_DIM, (h + 1) * HEAD_DIM)
        s = s_ref[h]
        ah, bh, kh, rh, vh = a_t[:, sl], b_t[:, sl], k_t[:, sl], r_t[:, sl], v[:, sl]
        l_ab = jnp.where(strict, _dot_nt(ah, bh, hi), 0.0)
        l_ak = jnp.where(strict, _dot_nt(ah, kh, hi), 0.0)
        m_rb = jnp.where(incl, _dot_nt(rh, bh, hi), 0.0)
        m_rk = jnp.where(incl, _dot_nt(rh, kh, hi), 0.0)
        inv = eye + l_ab
        pw = l_ab
        span = 1
        while span * 2 < n:
            pw = _dot_hi(pw, pw)
            inv = inv + _dot_hi(inv, pw)
            span *= 2
        u = _dot_hi(inv, _dot_nt(ah, s, hi) + _dot_hi(l_ak, vh))
        y_ref[0, :, sl] = _dot_nt(rh, s, hi) + _dot_hi(m_rb, u) + _dot_hi(m_rk, vh)
        s_ref[h] = s * g_end[:, sl] + _dot_tn(u, b_h[:, sl], hi) + _dot_tn(vh, k_h[:, sl], hi)

    @pl.when(c == nc - 1)
    def _():
        sout_ref[0] = s_ref[...]


def _wkv(r, lw, k, v, kk, a, s0):
    b, t, _ = r.shape
    nc = t // WKV_CHUNK
    tok = pl.BlockSpec((1, WKV_CHUNK, WIDTH), lambda i, c: (i, c, 0))
    st = pl.BlockSpec((1, N_HEADS, HEAD_DIM, HEAD_DIM), lambda i, c: (i, 0, 0, 0))
    return pl.pallas_call(
        functools.partial(_wkv_body, nc=nc),
        grid=(b, nc),
        in_specs=[tok] * 6 + [st],
        out_specs=[tok, st],
        out_shape=[jax.ShapeDtypeStruct((b, t, WIDTH), F32),
                   jax.ShapeDtypeStruct((b, N_HEADS, HEAD_DIM, HEAD_DIM), F32)],
        scratch_shapes=[pltpu.VMEM((N_HEADS, HEAD_DIM, HEAD_DIM), F32)],
        compiler_params=_params(2),
        name="wkv_chunked",
    )(r, lw, k, v, kk, a, s0)


def _attn_prompt_body(q_ref, k_ref, v_ref, cos_ref, sin_ref, o_ref, lse_ref, kwin_ref,
                      kprev_ref, vprev_ref, *, nb):
    n = pl.program_id(2)
    cos = cos_ref[...]
    sin = sin_ref[...]
    q = (_rope(q_ref[0], cos, sin) * (HEAD_DIM ** -0.5)).astype(BF16)
    k_rot = _rope(k_ref[0], cos, sin)

    slot = n % 2

    @pl.when(n == 0)
    def _():
        kprev_ref[1] = jnp.zeros((BLK, WIDTH), BF16)
        vprev_ref[1] = jnp.zeros((BLK, WIDTH), BF16)

    @pl.when(n == nb - 1)
    def _():
        kwin_ref[0] = k_rot

    kb = k_rot.astype(BF16)
    vb = v_ref[0].astype(BF16)
    kp = kprev_ref[1 - slot]
    vp = vprev_ref[1 - slot]
    qi = lax.broadcasted_iota(jnp.int32, (BLK, BLK), 0)
    kj = lax.broadcasted_iota(jnp.int32, (BLK, BLK), 1)
    prev_ok = (kj - jnp.where(n > 0, 0, BLK)) >= qi
    cur_ok = kj <= qi
    for h in range(N_HEADS):
        sl = slice(h * HEAD_DIM, (h + 1) * HEAD_DIM)
        s_prev = jnp.where(prev_ok, _dot_nt(q[:, sl], kp[:, sl]), NEG_INF)
        s_cur = jnp.where(cur_ok, _dot_nt(q[:, sl], kb[:, sl]), NEG_INF)
        mx = jnp.maximum(jnp.max(s_prev, axis=-1, keepdims=True),
                         jnp.max(s_cur, axis=-1, keepdims=True))
        p_prev = jnp.exp(s_prev - mx)
        p_cur = jnp.exp(s_cur - mx)
        den = jnp.sum(p_prev, axis=-1, keepdims=True) + jnp.sum(p_cur, axis=-1, keepdims=True)
        acc = _dot(p_prev.astype(BF16), vp[:, sl]) + _dot(p_cur.astype(BF16), vb[:, sl])
        o_ref[0, :, sl] = acc / den
        lse_ref[0, :, sl] = jnp.broadcast_to(mx + jnp.log(den), (BLK, HEAD_DIM))
    kprev_ref[slot] = kb
    vprev_ref[slot] = vb


def _attn_prompt(p_att, cos, sin, b, s, g):
    window, dil = ATT_GROUPS[g]
    assert window // dil == BLK and s % (BLK * dil) == 0
    l = s // dil
    nb = l // BLK
    ncol = C_ATT // WIDTH
    pv = p_att.reshape(b, l, dil * C_ATT)
    tab = lambda a: a.reshape(l, dil * WIDTH)
    blk = lambda part: pl.BlockSpec(
        (1, BLK, WIDTH), lambda i, r, n: (i, n, r * ncol + part * N_GROUPS + g))
    tspec = pl.BlockSpec((BLK, WIDTH), lambda i, r, n: (n, r))
    ospec = pl.BlockSpec((1, BLK, WIDTH), lambda i, r, n: (i, n, r))
    o, lse, kwin = pl.pallas_call(
        functools.partial(_attn_prompt_body, nb=nb),
        grid=(b, dil, nb),
        in_specs=[blk(0), blk(1), blk(2), tspec, tspec],
        out_specs=[ospec, ospec, pl.BlockSpec((1, BLK, WIDTH), lambda i, r, n: (i, 0, r))],
        out_shape=[jax.ShapeDtypeStruct((b, l, dil * WIDTH), F32),
                   jax.ShapeDtypeStruct((b, l, dil * WIDTH), F32),
                   jax.ShapeDtypeStruct((b, BLK, dil * WIDTH), F32)],
        scratch_shapes=[pltpu.VMEM((2, BLK, WIDTH), BF16), pltpu.VMEM((2, BLK, WIDTH), BF16)],
        compiler_params=_params(3),
        name=f"attn_prompt_g{g}",
    )(pv, pv, pv, tab(cos), tab(sin))
    return o.reshape(b * s, WIDTH), lse.reshape(b * s, WIDTH), kwin.reshape(b, window, WIDTH)


Q_PAD = 16


def _attn_sample_body(q_ref, k_ref, v_ref, cos_ref, sin_ref, c_ref,
                      o_ref, lse_ref, cn_ref, kall_ref, vall_ref, *, wb, window, dil, t):
    cos = cos_ref[...]
    sin = sin_ref[...]
    q = _rope(q_ref[0], cos, sin) * (HEAD_DIM ** -0.5)
    q = jnp.concatenate([q, jnp.zeros((Q_PAD - t, WIDTH), F32)], axis=0).astype(BF16)
    k_rot = _rope(k_ref[0], cos, sin)
    v_new = v_ref[0]
    cn_ref[0, 0, 0:wb - t] = c_ref[0, 0, t:wb]
    cn_ref[0, 0, wb - t:wb] = k_rot
    cn_ref[0, 1, 0:wb - t] = c_ref[0, 1, t:wb]
    cn_ref[0, 1, wb - t:wb] = v_new
    kall_ref[0:wb - t] = c_ref[0, 0, t:wb].astype(BF16)
    kall_ref[wb - t:wb] = k_rot.astype(BF16)
    vall_ref[0:wb - t] = c_ref[0, 1, t:wb].astype(BF16)
    vall_ref[wb - t:wb] = v_new.astype(BF16)
    k_old = c_ref[0, 0, 0:BLK].astype(BF16)
    v_old = c_ref[0, 1, 0:BLK].astype(BF16)
    tq = lax.broadcasted_iota(jnp.int32, (Q_PAD, wb), 0)
    ki = lax.broadcasted_iota(jnp.int32, (Q_PAD, wb), 1)
    dist = wb - t + tq - ki
    main_ok = (dist >= 0) & ((dist & (dil - 1)) == 0) & (dist <= window)
    tq = lax.broadcasted_iota(jnp.int32, (Q_PAD, BLK), 0)
    ki = lax.broadcasted_iota(jnp.int32, (Q_PAD, BLK), 1)
    dist = wb + tq - ki
    old_ok = (ki < t) & ((dist & (dil - 1)) == 0) & (dist <= window)
    for h in range(N_HEADS):
        sl = slice(h * HEAD_DIM, (h + 1) * HEAD_DIM)
        s_main = jnp.where(main_ok, _dot_nt(q[:, sl], kall_ref[:, sl]), NEG_INF)
        s_old = jnp.where(old_ok, _dot_nt(q[:, sl], k_old[:, sl]), NEG_INF)
        mx = jnp.maximum(jnp.max(s_main, axis=-1, keepdims=True),
                         jnp.max(s_old, axis=-1, keepdims=True))
        p_main = jnp.exp(s_main - mx)
        p_old = jnp.exp(s_old - mx)
        den = jnp.sum(p_main, axis=-1, keepdims=True) + jnp.sum(p_old, axis=-1, keepdims=True)
        acc = _dot(p_main.astype(BF16), vall_ref[:, sl]) + _dot(p_old.astype(BF16), v_old[:, sl])
        o_ref[0, :, sl] = (acc / den)[0:t]
        lse_ref[0, :, sl] = jnp.broadcast_to((mx + jnp.log(den))[0:t], (t, HEAD_DIM))


def _attn_sample(p_att, cos, sin, cache, b, t, g):
    window, dil = ATT_GROUPS[g]
    wb = cache.shape[2]
    assert wb >= BLK and wb % dil == 0 and t % 8 == 0 and t <= Q_PAD
    pv = p_att.reshape(b, t, C_ATT)
    cv = cache.reshape(b, 2, wb, WIDTH)
    blk = lambda part: pl.BlockSpec((1, t, WIDTH), lambda i: (i, 0, part * N_GROUPS + g))
    tspec = pl.BlockSpec((t, WIDTH), lambda i: (0, 0))
    cspec = pl.BlockSpec((1, 2, wb, WIDTH), lambda i: (i, 0, 0, 0))
    ospec = pl.BlockSpec((1, t, WIDTH), lambda i: (i, 0, 0))
    o, lse, new_cache = pl.pallas_call(
        functools.partial(_attn_sample_body, wb=wb, window=window, dil=dil, t=t),
        grid=(b,),
        in_specs=[blk(0), blk(1), blk(2), tspec, tspec, cspec],
        out_specs=[ospec, ospec, cspec],
        out_shape=[jax.ShapeDtypeStruct((b, t, WIDTH), F32), jax.ShapeDtypeStruct((b, t, WIDTH), F32),
                   jax.ShapeDtypeStruct((b, 2, wb, WIDTH), F32)],
        scratch_shapes=[pltpu.VMEM((wb, WIDTH), BF16), pltpu.VMEM((wb, WIDTH), BF16)],
        compiler_params=_params(1),
        name=f"attn_sample_g{g}",
    )(pv, pv, pv, cos, sin, cv)
    new_cache = new_cache.reshape(b, 2, wb, N_HEADS, HEAD_DIM)
    return o.reshape(b * t, WIDTH), lse.reshape(b * t, WIDTH), new_cache


def _merge_body(x_ref, pg_ref, y_ref, r_ref, k_ref, v_ref, g_ref, o0_ref, o1_ref, o2_ref,
                l0_ref, l1_ref, l2_ref, wa_ref, wb_ref, wo_ref, gng_ref, gnb_ref, rk_ref,
                lng_ref, lnb_ref, ones_ref, h_ref, *, alpha):
    ones_bd = ones_ref[...]
    y = y_ref[...]
    v = v_ref[...]
    mu = _head_sum(y, ones_bd) * (1.0 / HEAD_DIM)
    d = y - mu
    var = _head_sum(d * d, ones_bd) * (1.0 / HEAD_DIM)
    yn = d * lax.rsqrt(var + GN_EPS) * gng_ref[...] + gnb_ref[...]
    bonus = _head_sum(r_ref[...] * k_ref[...] * rk_ref[...], ones_bd) * v
    rw = (yn + bonus) * g_ref[...]
    l0, l1, l2 = l0_ref[...], l1_ref[...], l2_ref[...]
    mx = jnp.maximum(jnp.maximum(l0, l1), l2)
    e0, e1, e2 = jnp.exp(l0 - mx), jnp.exp(l1 - mx), jnp.exp(l2 - mx)
    att = (e0 * o0_ref[...] + e1 * o1_ref[...] + e2 * o2_ref[...]) / (e0 + e1 + e2)
    pg = pg_ref[...]
    merged = (_sigmoid(pg[:, :D_MODEL]) * _dot(rw.astype(BF16), wa_ref[...])
              + _sigmoid(pg[:, D_MODEL:]) * _dot(att.astype(BF16), wb_ref[...]))
    pre = alpha * x_ref[...] + _dot(merged.astype(BF16), wo_ref[...])
    h_ref[...] = _layer_norm(pre, lng_ref[...], lnb_ref[...])


def _merge(x, p_gate, y, r, k, v, g, outs, lses, lw, tm, alpha):
    m = x.shape[0]
    row = lambda c: pl.BlockSpec((tm, c), lambda i: (i, 0))
    const = lambda a: pl.BlockSpec(a.shape, lambda i: (0,) * a.ndim)
    consts = [lw["w_br_a"], lw["w_br_b"], lw["w_out"], lw["gn_g"], lw["gn_b"], lw["rk"],
              lw["ln1_g"], lw["ln1_b"], lw["ones_bd"]]
    return pl.pallas_call(
        functools.partial(_merge_body, alpha=alpha),
        grid=(m // tm,),
        in_specs=[row(D_MODEL), row(C_GATE)] + [row(WIDTH)] * 11 + [const(a) for a in consts],
        out_specs=row(D_MODEL),
        out_shape=jax.ShapeDtypeStruct((m, D_MODEL), F32),
        compiler_params=_params(1),
        name="merge_ln1",
    )(x, p_gate, y, r, k, v, g, *outs, *lses, *consts)


def _gelu(x):
    return 0.5 * x * (1.0 + lax.erf(x * (2.0 ** -0.5)))


def _ffn_body(*refs, seq, tm, nf, carry_mode, alpha):
    if carry_mode:
        (h_ref, wu_ref, wg_ref, wd_ref, cw_ref, cb_ref, lng_ref, lnb_ref, buf_ref,
         y_ref, cnew_ref, hb_ref, acc_ref, carry_ref) = refs
    else:
        (h_ref, wu_ref, wg_ref, wd_ref, cw_ref, cb_ref, lng_ref, lnb_ref, e1_ref, e2_ref,
         y_ref, u_ref, hb_ref, acc_ref) = refs
    i = pl.program_id(0)
    j = pl.program_id(1)

    @pl.when(j == 0)
    def _():
        hb_ref[...] = h_ref[...].astype(BF16)
        acc_ref[...] = jnp.zeros_like(acc_ref)

    hb = hb_ref[...]
    u = _dot(hb, wu_ref[...])
    gate = _dot(hb, wg_ref[...])
    row = lax.broadcasted_iota(jnp.int32, u.shape, 0)
    r1 = pltpu.roll(u, 1, 0)
    r2 = pltpu.roll(u, 2, 0)
    if carry_mode:
        slot = i % 2

        @pl.when(i % (seq // tm) == 0)
        def _():
            carry_ref[1 - slot, j, 6:8, :] = buf_ref[0]

        tail = carry_ref[1 - slot, j]
        u1 = jnp.where(row == 0, tail[7:8, :], r1)
        u2 = jnp.where(row == 0, tail[6:7, :], jnp.where(row == 1, tail[7:8, :], r2))
        carry_ref[slot, j] = u[tm - 8:tm, :]
        cnew_ref[0] = u[tm - 2:tm, :]
    else:
        u1 = jnp.where(row % seq == 0, e1_ref[...], r1)
        u2 = jnp.where(row % seq < 2, e2_ref[...], r2)
        u_ref[...] = u
    cw = cw_ref[...]
    conv = cb_ref[...] + cw[0:1, :] * u2 + cw[1:2, :] * u1 + cw[2:3, :] * u
    act = _gelu(conv) * gate
    acc_ref[...] += _dot(act.astype(BF16), wd_ref[...])

    @pl.when(j == nf - 1)
    def _():
        y_ref[...] = _layer_norm(alpha * h_ref[...] + acc_ref[...], lng_ref[...], lnb_ref[...])


def _conv_ffn(h, conv_buf, lw, seq, tm, carry_mode, alpha):
    m = h.shape[0]
    b = m // seq
    nf = D_FF // FF_CHUNK
    tf = FF_CHUNK
    const = lambda a: pl.BlockSpec(a.shape, lambda i, j: (0,) * a.ndim)
    in_specs = [pl.BlockSpec((tm, D_MODEL), lambda i, j: (i, 0)),
                pl.BlockSpec((D_MODEL, tf), lambda i, j: (0, j)),
                pl.BlockSpec((D_MODEL, tf), lambda i, j: (0, nf + j)),
                pl.BlockSpec((tf, D_MODEL), lambda i, j: (j, 0)),
                pl.BlockSpec((CONV_W, tf), lambda i, j: (0, j)),
                pl.BlockSpec((1, tf), lambda i, j: (0, j)),
                const(lw["ln2_g"]), const(lw["ln2_b"])]
    args = [h, lw["w_up"], lw["w_up"], lw["w_down"], lw["conv_w"], lw["conv_b"],
            lw["ln2_g"], lw["ln2_b"]]
    scratch = [pltpu.VMEM((tm, D_MODEL), BF16), pltpu.VMEM((tm, D_MODEL), F32)]
    y_spec = pl.BlockSpec((tm, D_MODEL), lambda i, j: (i, 0))
    y_shape = jax.ShapeDtypeStruct((m, D_MODEL), F32)
    if carry_mode:
        tps = seq // tm
        in_specs.append(pl.BlockSpec((1, CONV_W - 1, tf), lambda i, j: (i // tps, 0, j)))
        args.append(conv_buf)
        out_specs = [y_spec, pl.BlockSpec((1, CONV_W - 1, tf), lambda i, j: (i, 0, j))]
        out_shape = [y_shape, jax.ShapeDtypeStruct((m // tm, CONV_W - 1, D_FF), F32)]
        scratch.append(pltpu.VMEM((2, nf, 8, tf), F32))
    else:
        zeros = jnp.zeros((b, seq, D_FF), F32)
        e1 = zeros.at[:, 0].set(conv_buf[:, 1]).reshape(m, D_FF)
        e2 = zeros.at[:, 0].set(conv_buf[:, 0]).at[:, 1].set(conv_buf[:, 1]).reshape(m, D_FF)
        tile = pl.BlockSpec((tm, tf), lambda i, j: (i, j))
        in_specs += [tile, tile]
        args += [e1, e2]
        out_specs = [y_spec, tile]
        out_shape = [y_shape, jax.ShapeDtypeStruct((m, D_FF), F32)]
    y, aux = pl.pallas_call(
        functools.partial(_ffn_body, seq=seq, tm=tm, nf=nf, carry_mode=carry_mode, alpha=alpha),
        grid=(m // tm, nf),
        in_specs=in_specs,
        out_specs=out_specs,
        out_shape=out_shape,
        scratch_shapes=scratch,
        compiler_params=_params(2),
        name="conv_ffn_ln2",
    )(*args)
    if carry_mode:
        return y, aux[seq // tm - 1::seq // tm]
    return y, aux.reshape(b, seq, D_FF)[:, seq - (CONV_W - 1):]


def _rope_tables(pos):
    half = HEAD_DIM // 2
    inv = ROPE_THETA ** (-jnp.arange(half, dtype=F32) / half)
    ang = pos.astype(F32)[:, None] * inv[None, :]
    cos, sin = jnp.cos(ang), jnp.sin(ang)
    cos = jnp.tile(jnp.concatenate([cos, cos], axis=-1), (1, N_HEADS))
    sin = jnp.tile(jnp.concatenate([-sin, sin], axis=-1), (1, N_HEADS))
    return cos, sin


def _pad_cols(a, n):
    return jnp.pad(a, ((0, 0), (0, n - a.shape[1])))


def _layer_weights(l, w):
    row = lambda a: a.reshape(1, -1)
    z = lambda r, c: jnp.zeros((r, c), F32)
    w_in = w["w_in"][l]
    if l == 0:
        vres, mu_v = z(D_MODEL, D_MV_LORA), z(1, D_MV_LORA)
        v0, v2 = z(1, WIDTH), z(D_MV_LORA, WIDTH)
    else:
        vres, mu_v = w["w_in_vres"][l - 1], row(w["mu_vres"][l - 1])
        v0, v2 = row(w["rw_v0"][l - 1]), w["rw_v2"][l - 1]
    w_rw = _pad_cols(jnp.concatenate([w_in[:, C_ATT + C_GATE:], vres], axis=1), C_RW_PAD)
    mu = _pad_cols(jnp.concatenate([row(w["mu_rw"][l]), mu_v], axis=1), C_RW_PAD)
    w2a2 = jnp.concatenate([
        jnp.concatenate([w["rw_w2"][l], z(D_DECAY_LORA, WIDTH)], axis=1),
        jnp.concatenate([z(D_AAA_LORA, WIDTH), w["rw_a2"][l]], axis=1)], axis=0)
    pad_rows = 256 - D_GATE_LORA - D_MV_LORA
    g2v2 = jnp.concatenate([
        jnp.concatenate([w["rw_g2"][l], z(D_GATE_LORA, WIDTH)], axis=1),
        jnp.concatenate([z(D_MV_LORA, WIDTH), v2], axis=1),
        z(pad_rows, 2 * WIDTH)], axis=0)
    head = jnp.arange(WIDTH) // HEAD_DIM
    return {
        "w_att": w_in[:, :C_ATT].astype(BF16),
        "w_gate": w_in[:, C_ATT:C_ATT + C_GATE].astype(BF16),
        "w_rw": w_rw.astype(BF16),
        "mu": mu, "w0": row(w["rw_w0"][l]), "a0": row(w["rw_a0"][l]), "v0": v0,
        "kk": row(w["rw_kk"][l]), "ka": row(w["rw_ka"][l]),
        "w2a2": w2a2.astype(BF16), "g2v2": g2v2.astype(BF16),
        "ones_bd": (head[:, None] == head[None, :]).astype(BF16),
        "is_l0": jnp.full((1, WIDTH), 1.0 if l == 0 else 0.0, F32),
        "gn_g": row(w["rw_gn_g"][l]), "gn_b": row(w["rw_gn_b"][l]), "rk": row(w["rw_rk"][l]),
        "w_br_a": w["w_br_a"][l].astype(BF16), "w_br_b": w["w_br_b"][l].astype(BF16),
        "w_out": w["w_out"][l].astype(BF16),
        "ln1_g": row(w["ln1_g"][l]), "ln1_b": row(w["ln1_b"][l]),
        "w_up": w["ffn_w_up"][l].astype(BF16), "w_down": w["ffn_w_down"][l].astype(BF16),
        "conv_w": w["ffn_conv_w"][l], "conv_b": row(w["ffn_conv_b"][l]),
        "ln2_g": row(w["ln2_g"][l]), "ln2_b": row(w["ln2_b"][l]),
    }


def _trunk_layer(x, b, t, x_prev, wkv0, caches, conv_buf, v_first, lw, cos, sin, alpha):
    m = b * t
    prompt = caches is None
    tm = min(m, 1024 if prompt else 256)
    p_att = _mm(x, lw["w_att"], tm, WIDTH)
    p_gate = _mm(x, lw["w_gate"], tm, WIDTH)
    p_rw = _mm(x, lw["w_rw"], tm, C_RW_PAD // 3)
    bp = -(-b // 8) * 8
    prev0 = _mm(jnp.pad(x_prev, ((0, bp - b), (0, 0))), lw["w_rw"], bp, C_RW_PAD // 3)[:b]

    outs, lses, wins = [], [], []
    for g, (window, dil) in enumerate(ATT_GROUPS):
        if prompt:
            o, lse, kwin = _attn_prompt(p_att, cos, sin, b, t, g)
            lo = (2 * N_GROUPS + g) * WIDTH
            vwin = p_att.reshape(b, t, C_ATT)[:, t - window:, lo:lo + WIDTH]
            win = jnp.stack([kwin, vwin], axis=1).reshape(b, 2, window, N_HEADS, HEAD_DIM)
        else:
            o, lse, win = _attn_sample(p_att, cos, sin, caches[g], b, t, g)
        outs.append(o)
        lses.append(lse)
        wins.append(win)

    carry_mode = t >= 512
    tm_rw = 512 if carry_mode else m
    if carry_mode:
        prev_in = prev0.reshape(b, 1, C_RW_PAD)
    else:
        prev_in = jnp.repeat(prev0, t, axis=0)
    vf_in = jnp.zeros((m, WIDTH), F32) if v_first is None else v_first
    r, lgw, k, v, kk, a, g_out = _time_mix(p_rw, prev_in, vf_in, lw, t, tm_rw, carry_mode)
    if v_first is None:
        v_first = v

    tp = -(-t // WKV_CHUNK) * WKV_CHUNK
    tok = lambda a_: jnp.pad(a_.reshape(b, t, WIDTH), ((0, 0), (0, tp - t), (0, 0)))
    y, wkv_new = _wkv(tok(r), tok(lgw), tok(k), tok(v), tok(kk), tok(a), wkv0)
    y = y[:, :t].reshape(m, WIDTH)

    tm_mg = min(m, 256)
    h = _merge(x, p_gate, y, r, k, v, g_out, outs, lses, lw, tm_mg, alpha)
    y_out, conv_new = _conv_ffn(h, conv_buf, lw, t, 1024 if carry_mode else m, carry_mode, alpha)
    shift = x.reshape(b, t, D_MODEL)[:, -1]
    return y_out, wins, wkv_new, shift, conv_new, v_first


def kernel(x_prompt, x_sample, cache_win128, cache_win512, cache_win2048, state_wkv, state_shift, state_ffn_conv, w_in, w_in_vres, mu_rw, mu_vres, rw_w0, rw_w2, rw_a0, rw_a2, rw_g2, rw_v0, rw_v2, rw_kk, rw_ka, rw_rk, rw_gn_g, rw_gn_b, w_br_a, w_br_b, w_out, ln1_g, ln1_b, ffn_w_up, ffn_conv_w, ffn_conv_b, ffn_w_down, ln2_g, ln2_b):
    w = dict(w_in=w_in, w_in_vres=w_in_vres, mu_rw=mu_rw, mu_vres=mu_vres, rw_w0=rw_w0, rw_w2=rw_w2,
             rw_a0=rw_a0, rw_a2=rw_a2, rw_g2=rw_g2, rw_v0=rw_v0, rw_v2=rw_v2, rw_kk=rw_kk,
             rw_ka=rw_ka, rw_rk=rw_rk, rw_gn_g=rw_gn_g, rw_gn_b=rw_gn_b, w_br_a=w_br_a,
             w_br_b=w_br_b, w_out=w_out, ln1_g=ln1_g, ln1_b=ln1_b, ffn_w_up=ffn_w_up,
             ffn_conv_w=ffn_conv_w, ffn_conv_b=ffn_conv_b, ffn_w_down=ffn_w_down,
             ln2_g=ln2_g, ln2_b=ln2_b)
    caches = (cache_win128, cache_win512, cache_win2048)
    depth = w_in.shape[0]
    alpha = ALPHA
    bp, sp, _ = x_prompt.shape
    bs, ts, _ = x_sample.shape
    cos_p, sin_p = _rope_tables(jnp.arange(sp, dtype=jnp.int32))
    cos_s, sin_s = _rope_tables(PAST_LEN + jnp.arange(ts, dtype=jnp.int32))
    xp = x_prompt.reshape(bp * sp, D_MODEL)
    xs = x_sample.reshape(bs * ts, D_MODEL)
    vf_p = vf_s = None
    win_p = [[] for _ in ATT_GROUPS]
    win_s = [[] for _ in ATT_GROUPS]
    wkv_p, wkv_s, sh_p, sh_s, cv_p, cv_s = [], [], [], [], [], []
    for l in range(depth):
        lw = _layer_weights(l, w)
        xp, nw, s_new, sh, cv, vf_p = _trunk_layer(
            xp, bp, sp, jnp.zeros((bp, D_MODEL), F32),
            jnp.zeros((bp, N_HEADS, HEAD_DIM, HEAD_DIM), F32), None,
            jnp.zeros((bp, CONV_W - 1, D_FF), F32), vf_p, lw, cos_p, sin_p, alpha)
        for g in range(N_GROUPS):
            win_p[g].append(nw[g])
        wkv_p.append(s_new)
        sh_p.append(sh)
        cv_p.append(cv)
        xs, nw, s_new, sh, cv, vf_s = _trunk_layer(
            xs, bs, ts, state_shift[l], state_wkv[l], [c[l] for c in caches],
            state_ffn_conv[l], vf_s, lw, cos_s, sin_s, alpha)
        for g in range(N_GROUPS):
            win_s[g].append(nw[g])
        wkv_s.append(s_new)
        sh_s.append(sh)
        cv_s.append(cv)
    return (xp.reshape(bp, sp, D_MODEL), xs.reshape(bs, ts, D_MODEL),
            jnp.stack(win_p[0]), jnp.stack(win_s[0]), jnp.stack(win_p[1]), jnp.stack(win_s[1]),
            jnp.stack(win_p[2]), jnp.stack(win_s[2]), jnp.stack(wkv_p), jnp.stack(wkv_s),
            jnp.stack(sh_p), jnp.stack(sh_s), jnp.stack(cv_p), jnp.stack(cv_s))
```

```python
import functools

import jax
import jax.numpy as jnp
from jax import lax
from jax.experimental import pallas as pl
from jax.experimental.pallas import tpu as pltpu

F32 = jnp.float32
BF16 = jnp.bfloat16

D_MODEL = 1024
HEAD_DIM = 64
N_HEADS = 8
WIDTH = N_HEADS * HEAD_DIM
ATT_GROUPS = ((128, 1), (512, 4), (2048, 16))
N_GROUPS = len(ATT_GROUPS)
BLK = 128
ROPE_THETA = 10000.0
D_DECAY_LORA = 64
D_AAA_LORA = 64
D_GATE_LORA = 160
D_MV_LORA = 32
D_FF = 2816
CONV_W = 3
LN_EPS = 1e-5
GN_EPS = 64e-5
DEPTH = 4
ALPHA = (2.0 * DEPTH) ** 0.25
PAST_LEN = 8192
C_ATT = 3 * N_GROUPS * WIDTH
C_GATE = 2 * D_MODEL
C_RW = 3 * WIDTH + D_DECAY_LORA + D_AAA_LORA + D_GATE_LORA
C_RW_PAD = 1920
LORA_WA = 3 * WIDTH
LORA_GV = LORA_WA + 128
FF_CHUNK = 256
WKV_CHUNK = 64
ATT_ROWS = 2048
LANE = 128
VMEM_LIMIT = 56 * 1024 * 1024
NEG_INF = float("-inf")


def _params(n_axes):
    return pltpu.CompilerParams(
        dimension_semantics=("arbitrary",) * n_axes, vmem_limit_bytes=VMEM_LIMIT)


def _dot(a, b):
    return jnp.dot(a, b, preferred_element_type=F32)


NN = ((1,), (0,))
NT = ((1,), (1,))
TN = ((0,), (0,))


def _dg(a, b, dims):
    return lax.dot_general(a, b, (dims, ((), ())), preferred_element_type=F32)


def _dot_nt(a, b):
    return _dg(a, b, NT)


def _split(x):
    hi = x.astype(BF16)
    return hi, (x - hi.astype(F32)).astype(BF16)


def _dot_x3(a, b, dims=NN):
    ah, al = _split(a)
    bh, bl = _split(b)
    return _dg(ah, bh, dims) + _dg(ah, bl, dims) + _dg(al, bh, dims)


def _dot_x1(a, b, dims=NN):
    return _dg(a.astype(BF16), b.astype(BF16), dims)


def _head_sum(x, ones_bd):
    hi, lo = _split(x)
    return _dot(hi, ones_bd) + _dot(lo, ones_bd)


def _layer_norm(x, g, b):
    mu = jnp.mean(x, axis=-1, keepdims=True)
    d = x - mu
    var = jnp.mean(d * d, axis=-1, keepdims=True)
    return d * lax.rsqrt(var + LN_EPS) * g + b


def _sigmoid(x):
    return 1.0 / (1.0 + jnp.exp(-x))


def _rope(t, cos, sin):
    half = HEAD_DIM // 2
    outs = []
    for c in range(t.shape[-1] // LANE):
        tc = t[:, c * LANE:(c + 1) * LANE]
        lane = lax.broadcasted_iota(jnp.int32, tc.shape, 1)
        fwd = pltpu.roll(tc, LANE - half, 1)
        bwd = pltpu.roll(tc, half, 1)
        outs.append(jnp.where((lane & (HEAD_DIM - 1)) < half, fwd, bwd))
    partner = jnp.concatenate(outs, axis=-1)
    return t * cos + partner * sin


def _mm_body(x_ref, w_ref, o_ref, xb_ref):
    @pl.when(pl.program_id(1) == 0)
    def _():
        xb_ref[...] = x_ref[...].astype(BF16)

    o_ref[...] = _dot(xb_ref[...], w_ref[...])


def _mm(x, w, tm, tn):
    m, k = x.shape
    n = w.shape[1]
    return pl.pallas_call(
        _mm_body,
        grid=(m // tm, n // tn),
        in_specs=[pl.BlockSpec((tm, k), lambda i, j: (i, 0)),
                  pl.BlockSpec((k, tn), lambda i, j: (0, j))],
        out_specs=pl.BlockSpec((tm, tn), lambda i, j: (i, j)),
        out_shape=jax.ShapeDtypeStruct((m, n), F32),
        scratch_shapes=[pltpu.VMEM((tm, k), BF16)],
        compiler_params=_params(2),
        name="proj_mm",
    )(x, w)


def _tm_body(*refs, seq, tm, carry_mode):
    if carry_mode:
        (p_ref, prev0_ref, mu_ref, w0_ref, a0_ref, v0_ref, kkp_ref, kap_ref, w2a2_ref, g2v2_ref,
         ones_ref, vf_ref, l0_ref,
         r_out, lw_out, k_out, v_out, kk_out, a_out, g_out, carry_ref) = refs
    else:
        (p_ref, prev0_ref, mu_ref, w0_ref, a0_ref, v0_ref, kkp_ref, kap_ref, w2a2_ref, g2v2_ref,
         ones_ref, vf_ref, l0_ref,
         r_out, lw_out, k_out, v_out, kk_out, a_out, g_out) = refs
    p = p_ref[...]
    row = lax.broadcasted_iota(jnp.int32, p.shape, 0)
    rolled = pltpu.roll(p, 1, 0)
    if carry_mode:
        slot = pl.program_id(0) % 2

        @pl.when(pl.program_id(0) % (seq // tm) == 0)
        def _():
            carry_ref[1 - slot, 0:1, :] = prev0_ref[0]

        p_prev = jnp.where(row == 0, carry_ref[1 - slot, 0:1, :], rolled)
        carry_ref[slot, 0:1, :] = p[tm - 1:tm, :]
    else:
        p_prev = jnp.where(row % seq == 0, prev0_ref[...], rolled)
    z = p + (p_prev - p) * mu_ref[...]
    r = z[:, :WIDTH]
    k = z[:, WIDTH:2 * WIDTH]
    v = z[:, 2 * WIDTH:3 * WIDTH]
    wa = z[:, LORA_WA:LORA_WA + 128]
    lane = lax.broadcasted_iota(jnp.int32, wa.shape, 1)
    wa = jnp.where(lane < D_DECAY_LORA, jnp.tanh(wa), wa)
    wa_out = _dot(wa.astype(BF16), w2a2_ref[...])
    gv = z[:, LORA_GV:LORA_GV + 256]
    lane = lax.broadcasted_iota(jnp.int32, gv.shape, 1)
    gv = jnp.where(lane < D_GATE_LORA, _sigmoid(gv), gv)
    gv_out = _dot(gv.astype(BF16), g2v2_ref[...])
    x = -(w0_ref[...] + wa_out[:, :WIDTH])
    softplus = jnp.maximum(x, 0.0) + jnp.log1p(jnp.exp(-jnp.abs(x)))
    lw_out[...] = -jnp.exp(-softplus - 0.5)
    a = _sigmoid(a0_ref[...] + wa_out[:, WIDTH:])
    g_out[...] = gv_out[:, :WIDTH]
    vf = jnp.where(l0_ref[...] > 0.5, v, vf_ref[...])
    v = v + (vf - v) * _sigmoid(v0_ref[...] + gv_out[:, WIDTH:])
    kk = k * kkp_ref[...]
    norm = jnp.sqrt(_head_sum(kk * kk, ones_ref[...]))
    kk_out[...] = kk / jnp.maximum(norm, 1e-12)
    k_out[...] = k * (1.0 + (a - 1.0) * kap_ref[...])
    r_out[...] = r
    v_out[...] = v
    a_out[...] = a


def _time_mix(p_rw, prev0, vfirst, lw, seq, tm, carry_mode):
    m = p_rw.shape[0]
    row = lambda c: pl.BlockSpec((tm, c), lambda i: (i, 0))
    const = lambda a: pl.BlockSpec(a.shape, lambda i: (0,) * a.ndim)
    if carry_mode:
        prev_spec = pl.BlockSpec((1, 1, C_RW_PAD), lambda i: (i // (seq // tm), 0, 0))
        scratch = [pltpu.VMEM((2, 8, C_RW_PAD), F32)]
    else:
        prev_spec = row(C_RW_PAD)
        scratch = []
    consts = [lw["mu"], lw["w0"], lw["a0"], lw["v0"], lw["kk"], lw["ka"], lw["w2a2"], lw["g2v2"],
              lw["ones_bd"]]
    out = jax.ShapeDtypeStruct((m, WIDTH), F32)
    return pl.pallas_call(
        functools.partial(_tm_body, seq=seq, tm=tm, carry_mode=carry_mode),
        grid=(m // tm,),
        in_specs=[row(C_RW_PAD), prev_spec] + [const(a) for a in consts]
        + [row(WIDTH), const(lw["is_l0"])],
        out_specs=[row(WIDTH)] * 7,
        out_shape=[out] * 7,
        scratch_shapes=scratch,
        compiler_params=_params(1),
        name="time_mix",
    )(p_rw, prev0, *consts, vfirst, lw["is_l0"])


def _wkv_body(r_ref, lw_ref, k_ref, v_ref, kk_ref, a_ref, s0_ref, y_ref, sout_ref, s_ref, *, nc):
    c = pl.program_id(1)
    n = WKV_CHUNK

    @pl.when(c == 0)
    def _():
        s_ref[...] = s0_ref[0]

    lw = lw_ref[0]
    row = lax.broadcasted_iota(jnp.int32, (n, n), 0)
    col = lax.broadcasted_iota(jnp.int32, (n, n), 1)
    incl = col <= row
    strict = col < row
    eye = (col == row).astype(F32)
    tri = incl.astype(BF16)
    lw_hi = lw.astype(BF16)
    lw_mid, lw_lo = _split(lw - lw_hi.astype(F32))
    cum =_dot(tri, lw_hi) + _dot(tri, lw_mid) + _dot(tri, lw_lo)
    cum_end = cum[n - 1:n, :]
    kk = kk_ref[0]
    k = k_ref[0]
    kka = kk * a_ref[0]
    e_neg = jnp.exp(-cum)
    e_end = jnp.exp(cum_end - cum)
    a_t = -kk * jnp.exp(cum - lw)
    b_t = kka * e_neg
    k_t = k * e_neg
    r_t = r_ref[0] * jnp.exp(cum)
    b_h = kka * e_end
    k_h = k * e_end
    g_end = jnp.exp(cum_end)
    v = v_ref[0]
    heads = range(N_HEADS)
    hs = [slice(h * HEAD_DIM, (h + 1) * HEAD_DIM) for h in heads]
    s = [s_ref[h] for h in heads]
    ar = [jnp.concatenate([a_t[:, sl], r_t[:, sl]], axis=0) for sl in hs]
    bk = [jnp.concatenate([b_t[:, sl], k_t[:, sl]], axis=0) for sl in hs]
    gram = [_dot_x3(ar[h], bk[h], NT) for h in heads]
    on_s = [_dot_x1(ar[h], s[h], NT) for h in heads]
    l_ab = [jnp.where(strict, g[:n, :n], 0.0) for g in gram]
    l_ak = [jnp.where(strict, g[:n, n:], 0.0) for g in gram]
    m_rb = [jnp.where(incl, g[n:, :n], 0.0) for g in gram]
    m_rk = [jnp.where(incl, g[n:, n:], 0.0) for g in gram]
    rhs = [on_s[h][:n] + _dot_x1(l_ak[h], v[:, hs[h]]) for h in heads]
    acc = [eye + l for l in l_ab]
    pw = [_dot_x3(l, l) for l in l_ab]
    span = 2
    while span * 2 < n:
        both = [_dot_x3(jnp.concatenate([acc[h], pw[h]], axis=0), pw[h]) for h in heads]
        acc = [acc[h] + both[h][:n] for h in heads]
        pw = [both[h][n:] for h in heads]
        span *= 2
    acc = [acc[h] + _dot_x3(acc[h], pw[h]) for h in heads]
    u = [_dot_x3(acc[h], rhs[h]) for h in heads]
    uv = [jnp.concatenate([u[h], v[:, hs[h]]], axis=0) for h in heads]
    for h in heads:
        m_cat = jnp.concatenate([m_rb[h], m_rk[h]], axis=1)
        y_ref[0, :, hs[h]] = on_s[h][n:] + _dot_x1(m_cat, uv[h])
    for h in heads:
        bk_end = jnp.concatenate([b_h[:, hs[h]], k_h[:, hs[h]]], axis=0)
        s_ref[h] = s[h] * g_end[:, hs[h]] + _dot_x1(uv[h], bk_end, TN)

    @pl.when(c == nc - 1)
    def _():
        sout_ref[0] = s_ref[...]


def _wkv(r, lw, k, v, kk, a, s0):
    b, t, _ = r.shape
    nc = t // WKV_CHUNK
    tok = pl.BlockSpec((1, WKV_CHUNK, WIDTH), lambda i, c: (i, c, 0))
    st = pl.BlockSpec((1, N_HEADS, HEAD_DIM, HEAD_DIM), lambda i, c: (i, 0, 0, 0))
    return pl.pallas_call(
        functools.partial(_wkv_body, nc=nc),
        grid=(b, nc),
        in_specs=[tok] * 6 + [st],
        out_specs=[tok, st],
        out_shape=[jax.ShapeDtypeStruct((b, t, WIDTH), F32),
                   jax.ShapeDtypeStruct((b, N_HEADS, HEAD_DIM, HEAD_DIM), F32)],
        scratch_shapes=[pltpu.VMEM((N_HEADS, HEAD_DIM, HEAD_DIM), F32)],
        compiler_params=_params(2),
        name="wkv_chunked",
    )(r, lw, k, v, kk, a, s0)


def _attn_prompt_body(q_ref, k_ref, v_ref, cos_ref, sin_ref, o_ref, lse_ref, kwin_ref,
                      qs_ref, ks_ref, kc_ref, vc_ref, *, nblk, dil, window):
    j = pl.program_id(2)
    slot = j % 2
    cos = cos_ref[...]
    sin = sin_ref[...]
    qs_ref[...] = _rope(q_ref[0], cos, sin) * (HEAD_DIM ** -0.5)
    k_rot = _rope(k_ref[0], cos, sin)
    ks_ref[...] = k_rot

    @pl.when(j == nblk - 1)
    def _():
        kwin_ref[0] = k_rot[ATT_ROWS - window:, :]

    @pl.when(j == 0)
    def _():
        kc_ref[1] = jnp.zeros((dil, BLK, LANE), BF16)
        vc_ref[1] = jnp.zeros((dil, BLK, LANE), BF16)

    lane = lax.broadcasted_iota(jnp.int32, (BLK, LANE), 1)
    first = lane < HEAD_DIM
    qi = lax.broadcasted_iota(jnp.int32, (BLK, BLK), 0)
    kj = lax.broadcasted_iota(jnp.int32, (BLK, BLK), 1)
    prev_ok = kj >= qi
    prev_ok_first = (kj - jnp.where(j > 0, 0, BLK)) >= qi
    cur_ok = kj <= qi
    pair = range(2)
    for r in range(dil):
        kp = kc_ref[1 - slot, r]
        vp = vc_ref[1 - slot, r]
        for m in range(ATT_ROWS // (BLK * dil)):
            start = r + dil * BLK * m
            rows = pl.ds(start, BLK, stride=dil) if dil > 1 else pl.ds(start, BLK)
            qf = qs_ref[rows, :]
            kb = ks_ref[rows, :].astype(BF16)
            vb = v_ref[0, rows, :].astype(BF16)
            q2 = [jnp.where(first, qf, 0.0).astype(BF16), jnp.where(first, 0.0, qf).astype(BF16)]
            ok = prev_ok_first if m == 0 else prev_ok
            s_prev = [jnp.where(ok, _dot_nt(q2[h], kp), NEG_INF) for h in pair]
            s_cur = [jnp.where(cur_ok, _dot_nt(q2[h], kb), NEG_INF) for h in pair]
            mx = [jnp.maximum(jnp.max(s_prev[h], axis=-1, keepdims=True),
                              jnp.max(s_cur[h], axis=-1, keepdims=True)) for h in pair]
            p_prev = [jnp.exp(s_prev[h] - mx[h]) for h in pair]
            p_cur = [jnp.exp(s_cur[h] - mx[h]) for h in pair]
            den = [jnp.sum(p_prev[h], axis=-1, keepdims=True)
                   + jnp.sum(p_cur[h], axis=-1, keepdims=True) for h in pair]
            acc = [_dot(p_prev[h].astype(BF16), vp) + _dot(p_cur[h].astype(BF16), vb) for h in pair]
            lse = [mx[h] + jnp.log(den[h]) for h in pair]
            o_ref[0, rows, :] = jnp.where(first, acc[0] / den[0], acc[1] / den[1])
            lse_ref[0, rows, :] = jnp.where(first, lse[0], lse[1])
            kp, vp = kb, vb
        kc_ref[slot, r] = kp
        vc_ref[slot, r] = vp


def _attn_prompt(p_att, cos, sin, b, s, g):
    window, dil = ATT_GROUPS[g]
    assert window // dil == BLK and window <= ATT_ROWS and s % ATT_ROWS == 0
    nblk = s // ATT_ROWS
    pairs = WIDTH // LANE
    pv = p_att.reshape(b, s, C_ATT)
    blk = lambda part: pl.BlockSpec(
        (1, ATT_ROWS, LANE), lambda i, hp, j: (i, j, (part * N_GROUPS + g) * pairs + hp))
    tspec = pl.BlockSpec((ATT_ROWS, LANE), lambda i, hp, j: (j, 0))
    ospec = pl.BlockSpec((1, ATT_ROWS, LANE), lambda i, hp, j: (i, j, hp))
    o, lse, kwin = pl.pallas_call(
        functools.partial(_attn_prompt_body, nblk=nblk, dil=dil, window=window),
        grid=(b, pairs, nblk),
        in_specs=[blk(0), blk(1), blk(2), tspec, tspec],
        out_specs=[ospec, ospec, pl.BlockSpec((1, window, LANE), lambda i, hp, j: (i, 0, hp))],
        out_shape=[jax.ShapeDtypeStruct((b, s, WIDTH), F32),
                   jax.ShapeDtypeStruct((b, s, WIDTH), F32),
                   jax.ShapeDtypeStruct((b, window, WIDTH), F32)],
        scratch_shapes=[pltpu.VMEM((ATT_ROWS, LANE), F32), pltpu.VMEM((ATT_ROWS, LANE), F32),
                        pltpu.VMEM((2, dil, BLK, LANE), BF16), pltpu.VMEM((2, dil, BLK, LANE), BF16)],
        compiler_params=_params(3),
        name=f"attn_prompt_g{g}",
    )(pv, pv, pv, cos[:, :LANE], sin[:, :LANE])
    return o.reshape(b * s, WIDTH), lse.reshape(b * s, WIDTH), kwin


def _rope_sample_body(p_ref, cos_ref, sin_ref, o_ref):
    cos = cos_ref[...]
    sin = sin_ref[...]
    for part in range(3 * N_GROUPS):
        sl = slice(part * WIDTH, (part + 1) * WIDTH)
        x = p_ref[:, sl]
        if part < N_GROUPS:
            o_ref[:, sl] = _rope(x, cos, sin) * (HEAD_DIM ** -0.5)
        elif part < 2 * N_GROUPS:
            o_ref[:, sl] = _rope(x, cos, sin)
        else:
            o_ref[:, sl] = x


def _rope_sample(p_att, cos, sin):
    m = p_att.shape[0]
    full = lambda a: pl.BlockSpec(a.shape, lambda i: (0, 0))
    return pl.pallas_call(
        _rope_sample_body,
        grid=(1,),
        in_specs=[full(p_att), full(cos), full(sin)],
        out_specs=full(p_att),
        out_shape=jax.ShapeDtypeStruct((m, C_ATT), F32),
        compiler_params=_params(1),
        name="rope_sample",
    )(p_att, cos, sin)


def _cache_roll_body(c_ref, o_ref, sem, *, n, wb, t):
    i = pl.program_id(0)

    def copies(idx):
        slot = idx % 2
        return (pltpu.make_async_copy(c_ref.at[idx, :, pl.ds(t, wb - t)],
                                      o_ref.at[idx, :, pl.ds(0, wb - t)], sem.at[0, slot]),
                pltpu.make_async_copy(c_ref.at[idx, :, pl.ds(0, t)],
                                      o_ref.at[idx, :, pl.ds(wb - t, t)], sem.at[1, slot]))

    for cp in copies(i):
        cp.start()

    @pl.when(i > 0)
    def _():
        for cp in copies(i - 1):
            cp.wait()

    @pl.when(i == n - 1)
    def _():
        for cp in copies(i):
            cp.wait()


def _cache_roll(cache, t):
    depth, b, _, wb = cache.shape[:4]
    n = depth * b
    anyspec = pl.BlockSpec(memory_space=pl.ANY)
    flat = cache.reshape((n,) + cache.shape[2:])
    out = pl.pallas_call(
        functools.partial(_cache_roll_body, n=n, wb=wb, t=t),
        grid=(n,),
        in_specs=[anyspec],
        out_specs=anyspec,
        out_shape=jax.ShapeDtypeStruct(flat.shape, F32),
        scratch_shapes=[pltpu.SemaphoreType.DMA((2, 2))],
        compiler_params=_params(1),
        name="cache_roll",
    )(flat)
    return out.reshape(cache.shape)


def _attn_sample_body(q_ref, k_ref, v_ref, c_ref, _, o_ref, lse_ref, out_ref, sem, *,
                      layer, wb, dil, t):
    i = pl.program_id(0)
    copies = [
        pltpu.make_async_copy(k_ref.at[0, :, 0], out_ref.at[layer, i, 0, pl.ds(wb - t, t)], sem.at[0]),
        pltpu.make_async_copy(v_ref.at[0, :, 0], out_ref.at[layer, i, 1, pl.ds(wb - t, t)], sem.at[1]),
    ]
    for cp in copies:
        cp.start()
    rows = wb // dil
    for tq in range(t):
        r, i0 = tq % dil, tq // dil
        q = q_ref[0, tq, 0]
        kc = c_ref[0, 0, 0, i0:rows, r]
        vc = c_ref[0, 0, 1, i0:rows, r]
        new = [r + dil * m for m in range(i0 + 1)]
        s_c = jnp.sum(q[None] * kc, axis=-1, keepdims=True)
        s_n = [jnp.sum(q * k_ref[0, tn, 0], axis=-1, keepdims=True) for tn in new]
        mx = jnp.max(s_c, axis=0)
        for s in s_n:
            mx = jnp.maximum(mx, s)
        p_c = jnp.exp(s_c - mx[None])
        p_n = [jnp.exp(s - mx) for s in s_n]
        den = jnp.sum(p_c, axis=0)
        acc = jnp.sum(p_c * vc, axis=0)
        for p, tn in zip(p_n, new):
            den = den + p
            acc = acc + p * v_ref[0, tn, 0]
        o_ref[0, tq] = acc / den
        lse_ref[0, tq] = jnp.broadcast_to(mx + jnp.log(den), (N_HEADS, HEAD_DIM))
    for cp in copies:
        cp.wait()


def _attn_sample(p_nat, cache, stacked, layer, b, t, g):
    window, dil = ATT_GROUPS[g]
    depth, wb = cache.shape[0], cache.shape[3]
    assert wb == window and wb // dil == BLK and t <= wb
    ncls = min(dil, t)
    cls = cache.reshape(depth, b, 2, wb // dil, dil, N_HEADS, HEAD_DIM)
    blk = lambda part: pl.BlockSpec((1, t, 1, N_HEADS, HEAD_DIM),
                                    lambda i: (i, 0, part * N_GROUPS + g, 0, 0))
    cspec = pl.BlockSpec((1, 1, 2, wb // dil, ncls, N_HEADS, HEAD_DIM),
                         lambda i: (layer, i, 0, 0, 0, 0, 0))
    anyspec = pl.BlockSpec(memory_space=pl.ANY)
    ospec = pl.BlockSpec((1, t, N_HEADS, HEAD_DIM), lambda i: (i, 0, 0, 0))
    o, lse, stacked = pl.pallas_call(
        functools.partial(_attn_sample_body, layer=layer, wb=wb, dil=dil, t=t),
        grid=(b,),
        in_specs=[blk(0), blk(1), blk(2), cspec, anyspec],
        out_specs=[ospec, ospec, anyspec],
        out_shape=[jax.ShapeDtypeStruct((b, t, N_HEADS, HEAD_DIM), F32),
                   jax.ShapeDtypeStruct((b, t, N_HEADS, HEAD_DIM), F32),
                   jax.ShapeDtypeStruct(cache.shape, F32)],
        scratch_shapes=[pltpu.SemaphoreType.DMA((2,))],
        input_output_aliases={4: 2},
        compiler_params=_params(1),
        name=f"attn_sample_g{g}",
    )(p_nat, p_nat, p_nat, cls, stacked)
    return o.reshape(b * t, WIDTH), lse.reshape(b * t, WIDTH), stacked


def _merge_body(x_ref, pg_ref, y_ref, r_ref, k_ref, v_ref, g_ref, o0_ref, o1_ref, o2_ref,
                l0_ref, l1_ref, l2_ref, wa_ref, wb_ref, wo_ref, gng_ref, gnb_ref, rk_ref,
                lng_ref, lnb_ref, ones_ref, h_ref, *, alpha):
    ones_bd = ones_ref[...]
    y = y_ref[...]
    v = v_ref[...]
    mu = _head_sum(y, ones_bd) * (1.0 / HEAD_DIM)
    d = y - mu
    var = _head_sum(d * d, ones_bd) * (1.0 / HEAD_DIM)
    yn = d * lax.rsqrt(var + GN_EPS) * gng_ref[...] + gnb_ref[...]
    bonus = _head_sum(r_ref[...] * k_ref[...] * rk_ref[...], ones_bd) * v
    rw = (yn + bonus) * g_ref[...]
    l0, l1, l2 = l0_ref[...], l1_ref[...], l2_ref[...]
    mx = jnp.maximum(jnp.maximum(l0, l1), l2)
    e0, e1, e2 = jnp.exp(l0 - mx), jnp.exp(l1 - mx), jnp.exp(l2 - mx)
    att = (e0 * o0_ref[...] + e1 * o1_ref[...] + e2 * o2_ref[...]) / (e0 + e1 + e2)
    pg = pg_ref[...]
    merged = (_sigmoid(pg[:, :D_MODEL]) * _dot(rw.astype(BF16), wa_ref[...])
              + _sigmoid(pg[:, D_MODEL:]) * _dot(att.astype(BF16), wb_ref[...]))
    pre = alpha * x_ref[...] + _dot(merged.astype(BF16), wo_ref[...])
    h_ref[...] = _layer_norm(pre, lng_ref[...], lnb_ref[...])


def _merge(x, p_gate, y, r, k, v, g, outs, lses, lw, tm, alpha):
    m = x.shape[0]
    row = lambda c: pl.BlockSpec((tm, c), lambda i: (i, 0))
    const = lambda a: pl.BlockSpec(a.shape, lambda i: (0,) * a.ndim)
    consts = [lw["w_br_a"], lw["w_br_b"], lw["w_out"], lw["gn_g"], lw["gn_b"], lw["rk"],
              lw["ln1_g"], lw["ln1_b"], lw["ones_bd"]]
    return pl.pallas_call(
        functools.partial(_merge_body, alpha=alpha),
        grid=(m // tm,),
        in_specs=[row(D_MODEL), row(C_GATE)] + [row(WIDTH)] * 11 + [const(a) for a in consts],
        out_specs=row(D_MODEL),
        out_shape=jax.ShapeDtypeStruct((m, D_MODEL), F32),
        compiler_params=_params(1),
        name="merge_ln1",
    )(x, p_gate, y, r, k, v, g, *outs, *lses, *consts)


def _gelu(x):
    return 0.5 * x * (1.0 + lax.erf(x * (2.0 ** -0.5)))


def _ffn_body(*refs, seq, tm, nf, carry_mode, alpha):
    if carry_mode:
        (h_ref, wu_ref, wg_ref, wd_ref, cw_ref, cb_ref, lng_ref, lnb_ref, buf_ref,
         y_ref, cnew_ref, hb_ref, acc_ref, carry_ref) = refs
    else:
        (h_ref, wu_ref, wg_ref, wd_ref, cw_ref, cb_ref, lng_ref, lnb_ref, e1_ref, e2_ref,
         y_ref, u_ref, hb_ref, acc_ref) = refs
    i = pl.program_id(0)
    j = pl.program_id(1)

    @pl.when(j == 0)
    def _():
        hb_ref[...] = h_ref[...].astype(BF16)
        acc_ref[...] = jnp.zeros_like(acc_ref)

    hb = hb_ref[...]
    u = _dot(hb, wu_ref[...])
    gate = _dot(hb, wg_ref[...])
    row = lax.broadcasted_iota(jnp.int32, u.shape, 0)
    r1 = pltpu.roll(u, 1, 0)
    r2 = pltpu.roll(u, 2, 0)
    if carry_mode:
        slot = i % 2

        @pl.when(i % (seq // tm) == 0)
        def _():
            carry_ref[1 - slot, j, 6:8, :] = buf_ref[0]

        tail = carry_ref[1 - slot, j]
        u1 = jnp.where(row == 0, tail[7:8, :], r1)
        u2 = jnp.where(row == 0, tail[6:7, :], jnp.where(row == 1, tail[7:8, :], r2))
        carry_ref[slot, j] = u[tm - 8:tm, :]
        cnew_ref[0] = u[tm - 2:tm, :]
    else:
        u1 = jnp.where(row % seq == 0, e1_ref[...], r1)
        u2 = jnp.where(row % seq < 2, e2_ref[...], r2)
        u_ref[...] = u
    cw = cw_ref[...]
    conv = cb_ref[...] + cw[0:1, :] * u2 + cw[1:2, :] * u1 + cw[2:3, :] * u
    act = _gelu(conv) * gate
    acc_ref[...] += _dot(act.astype(BF16), wd_ref[...])

    @pl.when(j == nf - 1)
    def _():
        y_ref[...] = _layer_norm(alpha * h_ref[...] + acc_ref[...], lng_ref[...], lnb_ref[...])


def _conv_ffn(h, conv_buf, lw, seq, tm, carry_mode, alpha):
    m = h.shape[0]
    b = m // seq
    nf = D_FF // FF_CHUNK
    tf = FF_CHUNK
    const = lambda a: pl.BlockSpec(a.shape, lambda i, j: (0,) * a.ndim)
    in_specs = [pl.BlockSpec((tm, D_MODEL), lambda i, j: (i, 0)),
                pl.BlockSpec((D_MODEL, tf), lambda i, j: (0, j)),
                pl.BlockSpec((D_MODEL, tf), lambda i, j: (0, nf + j)),
                pl.BlockSpec((tf, D_MODEL), lambda i, j: (j, 0)),
                pl.BlockSpec((CONV_W, tf), lambda i, j: (0, j)),
                pl.BlockSpec((1, tf), lambda i, j: (0, j)),
                const(lw["ln2_g"]), const(lw["ln2_b"])]
    args = [h, lw["w_up"], lw["w_up"], lw["w_down"], lw["conv_w"], lw["conv_b"],
            lw["ln2_g"], lw["ln2_b"]]
    scratch = [pltpu.VMEM((tm, D_MODEL), BF16), pltpu.VMEM((tm, D_MODEL), F32)]
    y_spec = pl.BlockSpec((tm, D_MODEL), lambda i, j: (i, 0))
    y_shape = jax.ShapeDtypeStruct((m, D_MODEL), F32)
    if carry_mode:
        tps = seq // tm
        in_specs.append(pl.BlockSpec((1, CONV_W - 1, tf), lambda i, j: (i // tps, 0, j)))
        args.append(conv_buf)
        out_specs = [y_spec, pl.BlockSpec((1, CONV_W - 1, tf), lambda i, j: (i, 0, j))]
        out_shape = [y_shape, jax.ShapeDtypeStruct((m // tm, CONV_W - 1, D_FF), F32)]
        scratch.append(pltpu.VMEM((2, nf, 8, tf), F32))
    else:
        zeros = jnp.zeros((b, seq, D_FF), F32)
        e1 = zeros.at[:, 0].set(conv_buf[:, 1]).reshape(m, D_FF)
        e2 = zeros.at[:, 0].set(conv_buf[:, 0]).at[:, 1].set(conv_buf[:, 1]).reshape(m, D_FF)
        tile = pl.BlockSpec((tm, tf), lambda i, j: (i, j))
        in_specs += [tile, tile]
        args += [e1, e2]
        out_specs = [y_spec, tile]
        out_shape = [y_shape, jax.ShapeDtypeStruct((m, D_FF), F32)]
    y, aux = pl.pallas_call(
        functools.partial(_ffn_body, seq=seq, tm=tm, nf=nf, carry_mode=carry_mode, alpha=alpha),
        grid=(m // tm, nf),
        in_specs=in_specs,
        out_specs=out_specs,
        out_shape=out_shape,
        scratch_shapes=scratch,
        compiler_params=_params(2),
        name="conv_ffn_ln2",
    )(*args)
    if carry_mode:
        return y, aux[seq // tm - 1::seq // tm]
    return y, aux.reshape(b, seq, D_FF)[:, seq - (CONV_W - 1):]


def _rope_tables(pos):
    half = HEAD_DIM // 2
    inv = ROPE_THETA ** (-jnp.arange(half, dtype=F32) / half)
    ang = pos.astype(F32)[:, None] * inv[None, :]
    cos, sin = jnp.cos(ang), jnp.sin(ang)
    cos = jnp.tile(jnp.concatenate([cos, cos], axis=-1), (1, N_HEADS))
    sin = jnp.tile(jnp.concatenate([-sin, sin], axis=-1), (1, N_HEADS))
    return cos, sin


def _pad_cols(a, n):
    return jnp.pad(a, ((0, 0), (0, n - a.shape[1])))


def _layer_weights(l, w):
    row = lambda a: a.reshape(1, -1)
    z = lambda r, c: jnp.zeros((r, c), F32)
    w_in = w["w_in"][l]
    if l == 0:
        vres, mu_v = z(D_MODEL, D_MV_LORA), z(1, D_MV_LORA)
        v0, v2 = z(1, WIDTH), z(D_MV_LORA, WIDTH)
    else:
        vres, mu_v = w["w_in_vres"][l - 1], row(w["mu_vres"][l - 1])
        v0, v2 = row(w["rw_v0"][l - 1]), w["rw_v2"][l - 1]
    w_rw = _pad_cols(jnp.concatenate([w_in[:, C_ATT + C_GATE:], vres], axis=1), C_RW_PAD)
    mu = _pad_cols(jnp.concatenate([row(w["mu_rw"][l]), mu_v], axis=1), C_RW_PAD)
    w2a2 = jnp.concatenate([
        jnp.concatenate([w["rw_w2"][l], z(D_DECAY_LORA, WIDTH)], axis=1),
        jnp.concatenate([z(D_AAA_LORA, WIDTH), w["rw_a2"][l]], axis=1)], axis=0)
    pad_rows = 256 - D_GATE_LORA - D_MV_LORA
    g2v2 = jnp.concatenate([
        jnp.concatenate([w["rw_g2"][l], z(D_GATE_LORA, WIDTH)], axis=1),
        jnp.concatenate([z(D_MV_LORA, WIDTH), v2], axis=1),
        z(pad_rows, 2 * WIDTH)], axis=0)
    head = jnp.arange(WIDTH) // HEAD_DIM
    return {
        "w_att": w_in[:, :C_ATT].astype(BF16),
        "w_gate": w_in[:, C_ATT:C_ATT + C_GATE].astype(BF16),
        "w_rw": w_rw.astype(BF16),
        "mu": mu, "w0": row(w["rw_w0"][l]), "a0": row(w["rw_a0"][l]), "v0": v0,
        "kk": row(w["rw_kk"][l]), "ka": row(w["rw_ka"][l]),
        "w2a2": w2a2.astype(BF16), "g2v2": g2v2.astype(BF16),
        "ones_bd": (head[:, None] == head[None, :]).astype(BF16),
        "is_l0": jnp.full((1, WIDTH), 1.0 if l == 0 else 0.0, F32),
        "gn_g": row(w["rw_gn_g"][l]), "gn_b": row(w["rw_gn_b"][l]), "rk": row(w["rw_rk"][l]),
        "w_br_a": w["w_br_a"][l].astype(BF16), "w_br_b": w["w_br_b"][l].astype(BF16),
        "w_out": w["w_out"][l].astype(BF16),
        "ln1_g": row(w["ln1_g"][l]), "ln1_b": row(w["ln1_b"][l]),
        "w_up": w["ffn_w_up"][l].astype(BF16), "w_down": w["ffn_w_down"][l].astype(BF16),
        "conv_w": w["ffn_conv_w"][l], "conv_b": row(w["ffn_conv_b"][l]),
        "ln2_g": row(w["ln2_g"][l]), "ln2_b": row(w["ln2_b"][l]),
    }


def _trunk_layer(x, b, t, x_prev, wkv0, caches, conv_buf, v_first, lw, cos, sin, alpha, layer=0,
                 stacked=None):
    m = b * t
    prompt = caches is None
    tm = min(m, 1024 if prompt else 256)
    p_att = _mm(x, lw["w_att"], tm, WIDTH)
    p_gate = _mm(x, lw["w_gate"], tm, WIDTH)
    p_rw = _mm(x, lw["w_rw"], tm, C_RW_PAD // 3)
    bp = -(-b // 8) * 8
    prev0 = _mm(jnp.pad(x_prev, ((0, bp - b), (0, 0))), lw["w_rw"], bp, C_RW_PAD // 3)[:b]

    if not prompt:
        p_nat = _rope_sample(p_att, jnp.tile(cos, (b, 1)), jnp.tile(sin, (b, 1)))
        p_nat = p_nat.reshape(b, t, 3 * N_GROUPS, N_HEADS, HEAD_DIM)
    outs, lses, wins = [], [], []
    for g, (window, dil) in enumerate(ATT_GROUPS):
        if prompt:
            o, lse, kwin = _attn_prompt(p_att, cos, sin, b, t, g)
            lo = (2 * N_GROUPS + g) * WIDTH
            vwin = p_att.reshape(b, t, C_ATT)[:, t - window:, lo:lo + WIDTH]
            win = jnp.stack([kwin, vwin], axis=1).reshape(b, 2, window, N_HEADS, HEAD_DIM)
        else:
            o, lse, win = _attn_sample(p_nat, caches[g], stacked[g], layer, b, t, g)
        outs.append(o)
        lses.append(lse)
        wins.append(win)

    carry_mode = t >= 512
    tm_rw = 512 if carry_mode else m
    if carry_mode:
        prev_in = prev0.reshape(b, 1, C_RW_PAD)
    else:
        prev_in = jnp.repeat(prev0, t, axis=0)
    vf_in = jnp.zeros((m, WIDTH), F32) if v_first is None else v_first
    r, lgw, k, v, kk, a, g_out = _time_mix(p_rw, prev_in, vf_in, lw, t, tm_rw, carry_mode)
    if v_first is None:
        v_first = v

    tp = -(-t // WKV_CHUNK) * WKV_CHUNK
    tok = lambda a_: jnp.pad(a_.reshape(b, t, WIDTH), ((0, 0), (0, tp - t), (0, 0)))
    y, wkv_new = _wkv(tok(r), tok(lgw), tok(k), tok(v), tok(kk), tok(a), wkv0)
    y = y[:, :t].reshape(m, WIDTH)

    tm_mg = min(m, 256)
    h = _merge(x, p_gate, y, r, k, v, g_out, outs, lses, lw, tm_mg, alpha)
    y_out, conv_new = _conv_ffn(h, conv_buf, lw, t, 1024 if carry_mode else m, carry_mode, alpha)
    shift = x.reshape(b, t, D_MODEL)[:, -1]
    return y_out, wins, wkv_new, shift, conv_new, v_first


def kernel(x_prompt, x_sample, cache_win128, cache_win512, cache_win2048, state_wkv, state_shift, state_ffn_conv, w_in, w_in_vres, mu_rw, mu_vres, rw_w0, rw_w2, rw_a0, rw_a2, rw_g2, rw_v0, rw_v2, rw_kk, rw_ka, rw_rk, rw_gn_g, rw_gn_b, w_br_a, w_br_b, w_out, ln1_g, ln1_b, ffn_w_up, ffn_conv_w, ffn_conv_b, ffn_w_down, ln2_g, ln2_b):
    w = dict(w_in=w_in, w_in_vres=w_in_vres, mu_rw=mu_rw, mu_vres=mu_vres, rw_w0=rw_w0, rw_w2=rw_w2,
             rw_a0=rw_a0, rw_a2=rw_a2, rw_g2=rw_g2, rw_v0=rw_v0, rw_v2=rw_v2, rw_kk=rw_kk,
             rw_ka=rw_ka, rw_rk=rw_rk, rw_gn_g=rw_gn_g, rw_gn_b=rw_gn_b, w_br_a=w_br_a,
             w_br_b=w_br_b, w_out=w_out, ln1_g=ln1_g, ln1_b=ln1_b, ffn_w_up=ffn_w_up,
             ffn_conv_w=ffn_conv_w, ffn_conv_b=ffn_conv_b, ffn_w_down=ffn_w_down,
             ln2_g=ln2_g, ln2_b=ln2_b)
    caches = (cache_win128, cache_win512, cache_win2048)
    depth = w_in.shape[0]
    alpha = ALPHA
    bp, sp, _ = x_prompt.shape
    bs, ts, _ = x_sample.shape
    cos_p, sin_p = _rope_tables(jnp.arange(sp, dtype=jnp.int32))
    cos_s, sin_s = _rope_tables(PAST_LEN + jnp.arange(ts, dtype=jnp.int32))
    xp = x_prompt.reshape(bp * sp, D_MODEL)
    xs = x_sample.reshape(bs * ts, D_MODEL)
    vf_p = vf_s = None
    win_p = [[] for _ in ATT_GROUPS]
    win_s = [_cache_roll(c, ts) for c in caches]
    wkv_p, wkv_s, sh_p, sh_s, cv_p, cv_s = [], [], [], [], [], []
    for l in range(depth):
        lw = _layer_weights(l, w)
        xp, nw, s_new, sh, cv, vf_p = _trunk_layer(
            xp, bp, sp, jnp.zeros((bp, D_MODEL), F32),
            jnp.zeros((bp, N_HEADS, HEAD_DIM, HEAD_DIM), F32), None,
            jnp.zeros((bp, CONV_W - 1, D_FF), F32), vf_p, lw, cos_p, sin_p, alpha)
        for g in range(N_GROUPS):
            win_p[g].append(nw[g])
        wkv_p.append(s_new)
        sh_p.append(sh)
        cv_p.append(cv)
        xs, win_s, s_new, sh, cv, vf_s = _trunk_layer(
            xs, bs, ts, state_shift[l], state_wkv[l], caches,
            state_ffn_conv[l], vf_s, lw, cos_s, sin_s, alpha, l, win_s)
        wkv_s.append(s_new)
        sh_s.append(sh)
        cv_s.append(cv)
    return (xp.reshape(bp, sp, D_MODEL), xs.reshape(bs, ts, D_MODEL),
            jnp.stack(win_p[0]), win_s[0], jnp.stack(win_p[1]), win_s[1],
            jnp.stack(win_p[2]), win_s[2], jnp.stack(wkv_p), jnp.stack(wkv_s),
            jnp.stack(sh_p), jnp.stack(sh_s), jnp.stack(cv_p), jnp.stack(cv_s))
```

```python
import functools

import jax
import jax.numpy as jnp
from jax import lax
from jax.experimental import pallas as pl
from jax.experimental.pallas import tpu as pltpu

F32 = jnp.float32
BF16 = jnp.bfloat16

D_MODEL = 1024
HEAD_DIM = 64
N_HEADS = 8
WIDTH = N_HEADS * HEAD_DIM
ATT_GROUPS = ((128, 1), (512, 4), (2048, 16))
N_GROUPS = len(ATT_GROUPS)
BLK = 128
ROPE_THETA = 10000.0
D_DECAY_LORA = 64
D_AAA_LORA = 64
D_GATE_LORA = 160
D_MV_LORA = 32
D_FF = 2816
CONV_W = 3
LN_EPS = 1e-5
GN_EPS = 64e-5
DEPTH = 4
ALPHA = (2.0 * DEPTH) ** 0.25
PAST_LEN = 8192
C_ATT = 3 * N_GROUPS * WIDTH
C_GATE = 2 * D_MODEL
C_RW = 3 * WIDTH + D_DECAY_LORA + D_AAA_LORA + D_GATE_LORA
C_RW_PAD = 1920
LORA_WA = 3 * WIDTH
LORA_GV = LORA_WA + 128
FF_CHUNK = 256
WKV_CHUNK = 64
ATT_ROWS = 2048
ROLL_ROWS = 1024
LANE = 128
VMEM_LIMIT = 56 * 1024 * 1024
NEG_INF = float("-inf")


def _params(n_axes):
    return pltpu.CompilerParams(
        dimension_semantics=("arbitrary",) * n_axes, vmem_limit_bytes=VMEM_LIMIT)


def _dot(a, b):
    return jnp.dot(a, b, preferred_element_type=F32)


NN = ((1,), (0,))
NT = ((1,), (1,))
TN = ((0,), (0,))


def _dg(a, b, dims):
    return lax.dot_general(a, b, (dims, ((), ())), preferred_element_type=F32)


def _dot_nt(a, b):
    return _dg(a, b, NT)


def _split(x):
    hi = x.astype(BF16)
    return hi, (x - hi.astype(F32)).astype(BF16)


def _dot_x3(a, b, dims=NN):
    ah, al = _split(a)
    bh, bl = _split(b)
    return _dg(ah, bh, dims) + _dg(ah, bl, dims) + _dg(al, bh, dims)


def _dot_x1(a, b, dims=NN):
    return _dg(a.astype(BF16), b.astype(BF16), dims)


def _head_sum(x, ones_bd):
    hi, lo = _split(x)
    return _dot(hi, ones_bd) + _dot(lo, ones_bd)


def _layer_norm(x, g, b):
    mu = jnp.mean(x, axis=-1, keepdims=True)
    d = x - mu
    var = jnp.mean(d * d, axis=-1, keepdims=True)
    return d * lax.rsqrt(var + LN_EPS) * g + b


def _sigmoid(x):
    return 1.0 / (1.0 + jnp.exp(-x))


def _rope(t, cos, sin):
    half = HEAD_DIM // 2
    outs = []
    for c in range(t.shape[-1] // LANE):
        tc = t[:, c * LANE:(c + 1) * LANE]
        lane = lax.broadcasted_iota(jnp.int32, tc.shape, 1)
        fwd = pltpu.roll(tc, LANE - half, 1)
        bwd = pltpu.roll(tc, half, 1)
        outs.append(jnp.where((lane & (HEAD_DIM - 1)) < half, fwd, bwd))
    partner = jnp.concatenate(outs, axis=-1)
    return t * cos + partner * sin


def _mm_body(x_ref, w_ref, o_ref, xb_ref):
    @pl.when(pl.program_id(1) == 0)
    def _():
        xb_ref[...] = x_ref[...].astype(BF16)

    o_ref[...] = _dot(xb_ref[...], w_ref[...])


def _mm(x, w, tm, tn):
    m, k = x.shape
    n = w.shape[1]
    return pl.pallas_call(
        _mm_body,
        grid=(m // tm, n // tn),
        in_specs=[pl.BlockSpec((tm, k), lambda i, j: (i, 0)),
                  pl.BlockSpec((k, tn), lambda i, j: (0, j))],
        out_specs=pl.BlockSpec((tm, tn), lambda i, j: (i, j)),
        out_shape=jax.ShapeDtypeStruct((m, n), F32),
        scratch_shapes=[pltpu.VMEM((tm, k), BF16)],
        compiler_params=_params(2),
        name="proj_mm",
    )(x, w)


def _tm_body(*refs, seq, tm, carry_mode):
    if carry_mode:
        (p_ref, prev0_ref, mu_ref, w0_ref, a0_ref, v0_ref, kkp_ref, kap_ref, w2a2_ref, g2v2_ref,
         ones_ref, vf_ref, l0_ref,
         r_out, lw_out, k_out, v_out, kk_out, a_out, g_out, carry_ref) = refs
    else:
        (p_ref, prev0_ref, mu_ref, w0_ref, a0_ref, v0_ref, kkp_ref, kap_ref, w2a2_ref, g2v2_ref,
         ones_ref, vf_ref, l0_ref,
         r_out, lw_out, k_out, v_out, kk_out, a_out, g_out) = refs
    p = p_ref[...]
    row = lax.broadcasted_iota(jnp.int32, p.shape, 0)
    rolled = pltpu.roll(p, 1, 0)
    if carry_mode:
        slot = pl.program_id(0) % 2

        @pl.when(pl.program_id(0) % (seq // tm) == 0)
        def _():
            carry_ref[1 - slot, 0:1, :] = prev0_ref[0]

        p_prev = jnp.where(row == 0, carry_ref[1 - slot, 0:1, :], rolled)
        carry_ref[slot, 0:1, :] = p[tm - 1:tm, :]
    else:
        p_prev = jnp.where(row % seq == 0, prev0_ref[...], rolled)
    z = p + (p_prev - p) * mu_ref[...]
    r = z[:, :WIDTH]
    k = z[:, WIDTH:2 * WIDTH]
    v = z[:, 2 * WIDTH:3 * WIDTH]
    wa = z[:, LORA_WA:LORA_WA + 128]
    lane = lax.broadcasted_iota(jnp.int32, wa.shape, 1)
    wa = jnp.where(lane < D_DECAY_LORA, jnp.tanh(wa), wa)
    wa_out = _dot(wa.astype(BF16), w2a2_ref[...])
    gv = z[:, LORA_GV:LORA_GV + 256]
    lane = lax.broadcasted_iota(jnp.int32, gv.shape, 1)
    gv = jnp.where(lane < D_GATE_LORA, _sigmoid(gv), gv)
    gv_out = _dot(gv.astype(BF16), g2v2_ref[...])
    x = -(w0_ref[...] + wa_out[:, :WIDTH])
    softplus = jnp.maximum(x, 0.0) + jnp.log1p(jnp.exp(-jnp.abs(x)))
    lw_out[...] = -jnp.exp(-softplus - 0.5)
    a = _sigmoid(a0_ref[...] + wa_out[:, WIDTH:])
    g_out[...] = gv_out[:, :WIDTH]
    vf = jnp.where(l0_ref[...] > 0.5, v, vf_ref[...])
    v = v + (vf - v) * _sigmoid(v0_ref[...] + gv_out[:, WIDTH:])
    kk = k * kkp_ref[...]
    norm = jnp.sqrt(_head_sum(kk * kk, ones_ref[...]))
    kk_out[...] = kk / jnp.maximum(norm, 1e-12)
    k_out[...] = k * (1.0 + (a - 1.0) * kap_ref[...])
    r_out[...] = r
    v_out[...] = v
    a_out[...] = a


def _time_mix(p_rw, prev0, vfirst, lw, seq, tm, carry_mode):
    m = p_rw.shape[0]
    row = lambda c: pl.BlockSpec((tm, c), lambda i: (i, 0))
    const = lambda a: pl.BlockSpec(a.shape, lambda i: (0,) * a.ndim)
    if carry_mode:
        prev_spec = pl.BlockSpec((1, 1, C_RW_PAD), lambda i: (i // (seq // tm), 0, 0))
        scratch = [pltpu.VMEM((2, 8, C_RW_PAD), F32)]
    else:
        prev_spec = row(C_RW_PAD)
        scratch = []
    consts = [lw["mu"], lw["w0"], lw["a0"], lw["v0"], lw["kk"], lw["ka"], lw["w2a2"], lw["g2v2"],
              lw["ones_bd"]]
    out = jax.ShapeDtypeStruct((m, WIDTH), F32)
    return pl.pallas_call(
        functools.partial(_tm_body, seq=seq, tm=tm, carry_mode=carry_mode),
        grid=(m // tm,),
        in_specs=[row(C_RW_PAD), prev_spec] + [const(a) for a in consts]
        + [row(WIDTH), const(lw["is_l0"])],
        out_specs=[row(WIDTH)] * 7,
        out_shape=[out] * 7,
        scratch_shapes=scratch,
        compiler_params=_params(1),
        name="time_mix",
    )(p_rw, prev0, *consts, vfirst, lw["is_l0"])


def _wkv_body(r_ref, lw_ref, k_ref, v_ref, kk_ref, a_ref, s0_ref, y_ref, sout_ref, s_ref, *, nc):
    c = pl.program_id(1)
    n = WKV_CHUNK

    @pl.when(c == 0)
    def _():
        s_ref[...] = s0_ref[0]

    lw = lw_ref[0]
    row = lax.broadcasted_iota(jnp.int32, (n, n), 0)
    col = lax.broadcasted_iota(jnp.int32, (n, n), 1)
    incl = col <= row
    strict = col < row
    eye = (col == row).astype(F32)
    tri = incl.astype(BF16)
    lw_hi = lw.astype(BF16)
    lw_mid, lw_lo = _split(lw - lw_hi.astype(F32))
    cum =_dot(tri, lw_hi) + _dot(tri, lw_mid) + _dot(tri, lw_lo)
    cum_end = cum[n - 1:n, :]
    kk = kk_ref[0]
    k = k_ref[0]
    kka = kk * a_ref[0]
    e_neg = jnp.exp(-cum)
    e_end = jnp.exp(cum_end - cum)
    a_t = -kk * jnp.exp(cum - lw)
    b_t = kka * e_neg
    k_t = k * e_neg
    r_t = r_ref[0] * jnp.exp(cum)
    b_h = kka * e_end
    k_h = k * e_end
    g_end = jnp.exp(cum_end)
    v = v_ref[0]
    heads = range(N_HEADS)
    hs = [slice(h * HEAD_DIM, (h + 1) * HEAD_DIM) for h in heads]
    s = [s_ref[h] for h in heads]
    ar = [jnp.concatenate([a_t[:, sl], r_t[:, sl]], axis=0) for sl in hs]
    bk = [jnp.concatenate([b_t[:, sl], k_t[:, sl]], axis=0) for sl in hs]
    gram = [_dot_x3(ar[h], bk[h], NT) for h in heads]
    on_s = [_dot_x1(ar[h], s[h], NT) for h in heads]
    l_ab = [jnp.where(strict, g[:n, :n], 0.0) for g in gram]
    l_ak = [jnp.where(strict, g[:n, n:], 0.0) for g in gram]
    m_rb = [jnp.where(incl, g[n:, :n], 0.0) for g in gram]
    m_rk = [jnp.where(incl, g[n:, n:], 0.0) for g in gram]
    rhs = [on_s[h][:n] + _dot_x1(l_ak[h], v[:, hs[h]]) for h in heads]
    acc = [eye + l for l in l_ab]
    pw = [_dot_x3(l, l) for l in l_ab]
    span = 2
    while span * 2 < n:
        both = [_dot_x3(jnp.concatenate([acc[h], pw[h]], axis=0), pw[h]) for h in heads]
        acc = [acc[h] + both[h][:n] for h in heads]
        pw = [both[h][n:] for h in heads]
        span *= 2
    acc = [acc[h] + _dot_x3(acc[h], pw[h]) for h in heads]
    u = [_dot_x3(acc[h], rhs[h]) for h in heads]
    uv = [jnp.concatenate([u[h], v[:, hs[h]]], axis=0) for h in heads]
    for h in heads:
        m_cat = jnp.concatenate([m_rb[h], m_rk[h]], axis=1)
        y_ref[0, :, hs[h]] = on_s[h][n:] + _dot_x1(m_cat, uv[h])
    for h in heads:
        bk_end = jnp.concatenate([b_h[:, hs[h]], k_h[:, hs[h]]], axis=0)
        s_ref[h] = s[h] * g_end[:, hs[h]] + _dot_x1(uv[h], bk_end, TN)

    @pl.when(c == nc - 1)
    def _():
        sout_ref[0] = s_ref[...]


def _wkv(r, lw, k, v, kk, a, s0):
    b, t, _ = r.shape
    nc = t // WKV_CHUNK
    tok = pl.BlockSpec((1, WKV_CHUNK, WIDTH), lambda i, c: (i, c, 0))
    st = pl.BlockSpec((1, N_HEADS, HEAD_DIM, HEAD_DIM), lambda i, c: (i, 0, 0, 0))
    return pl.pallas_call(
        functools.partial(_wkv_body, nc=nc),
        grid=(b, nc),
        in_specs=[tok] * 6 + [st],
        out_specs=[tok, st],
        out_shape=[jax.ShapeDtypeStruct((b, t, WIDTH), F32),
                   jax.ShapeDtypeStruct((b, N_HEADS, HEAD_DIM, HEAD_DIM), F32)],
        scratch_shapes=[pltpu.VMEM((N_HEADS, HEAD_DIM, HEAD_DIM), F32)],
        compiler_params=_params(2),
        name="wkv_chunked",
    )(r, lw, k, v, kk, a, s0)


def _attn_prompt_body(q_ref, k_ref, v_ref, cos_ref, sin_ref, o_ref, lse_ref, kwin_ref,
                      qs_ref, ks_ref, kc_ref, vc_ref, *, nblk, dil, window):
    j = pl.program_id(2)
    slot = j % 2
    cos = cos_ref[...]
    sin = sin_ref[...]
    qs_ref[...] = _rope(q_ref[0], cos, sin) * (HEAD_DIM ** -0.5)
    k_rot = _rope(k_ref[0], cos, sin)
    ks_ref[...] = k_rot

    @pl.when(j == nblk - 1)
    def _():
        kwin_ref[0] = k_rot[ATT_ROWS - window:, :]

    @pl.when(j == 0)
    def _():
        kc_ref[1] = jnp.zeros((dil, BLK, LANE), BF16)
        vc_ref[1] = jnp.zeros((dil, BLK, LANE), BF16)

    lane = lax.broadcasted_iota(jnp.int32, (BLK, LANE), 1)
    first = lane < HEAD_DIM
    qi = lax.broadcasted_iota(jnp.int32, (BLK, BLK), 0)
    kj = lax.broadcasted_iota(jnp.int32, (BLK, BLK), 1)
    prev_ok = kj >= qi
    prev_ok_first = (kj - jnp.where(j > 0, 0, BLK)) >= qi
    cur_ok = kj <= qi
    pair = range(2)
    for r in range(dil):
        kp = kc_ref[1 - slot, r]
        vp = vc_ref[1 - slot, r]
        for m in range(ATT_ROWS // (BLK * dil)):
            start = r + dil * BLK * m
            rows = pl.ds(start, BLK, stride=dil) if dil > 1 else pl.ds(start, BLK)
            qf = qs_ref[rows, :]
            kb = ks_ref[rows, :].astype(BF16)
            vb = v_ref[0, rows, :].astype(BF16)
            q2 = [jnp.where(first, qf, 0.0).astype(BF16), jnp.where(first, 0.0, qf).astype(BF16)]
            ok = prev_ok_first if m == 0 else prev_ok
            s_prev = [jnp.where(ok, _dot_nt(q2[h], kp), NEG_INF) for h in pair]
            s_cur = [jnp.where(cur_ok, _dot_nt(q2[h], kb), NEG_INF) for h in pair]
            mx = [jnp.maximum(jnp.max(s_prev[h], axis=-1, keepdims=True),
                              jnp.max(s_cur[h], axis=-1, keepdims=True)) for h in pair]
            p_prev = [jnp.exp(s_prev[h] - mx[h]) for h in pair]
            p_cur = [jnp.exp(s_cur[h] - mx[h]) for h in pair]
            den = [jnp.sum(p_prev[h], axis=-1, keepdims=True)
                   + jnp.sum(p_cur[h], axis=-1, keepdims=True) for h in pair]
            acc = [_dot(p_prev[h].astype(BF16), vp) + _dot(p_cur[h].astype(BF16), vb) for h in pair]
            lse = [mx[h] + jnp.log(den[h]) for h in pair]
            o_ref[0, rows, :] = jnp.where(first, acc[0] / den[0], acc[1] / den[1])
            lse_ref[0, rows, :] = jnp.where(first, lse[0], lse[1])
            kp, vp = kb, vb
        kc_ref[slot, r] = kp
        vc_ref[slot, r] = vp


def _attn_prompt(p_att, cos, sin, b, s, g):
    window, dil = ATT_GROUPS[g]
    assert window // dil == BLK and window <= ATT_ROWS and s % ATT_ROWS == 0
    nblk = s // ATT_ROWS
    pairs = WIDTH // LANE
    pv = p_att.reshape(b, s, C_ATT)
    blk = lambda part: pl.BlockSpec(
        (1, ATT_ROWS, LANE), lambda i, hp, j: (i, j, (part * N_GROUPS + g) * pairs + hp))
    tspec = pl.BlockSpec((ATT_ROWS, LANE), lambda i, hp, j: (j, 0))
    ospec = pl.BlockSpec((1, ATT_ROWS, LANE), lambda i, hp, j: (i, j, hp))
    o, lse, kwin = pl.pallas_call(
        functools.partial(_attn_prompt_body, nblk=nblk, dil=dil, window=window),
        grid=(b, pairs, nblk),
        in_specs=[blk(0), blk(1), blk(2), tspec, tspec],
        out_specs=[ospec, ospec, pl.BlockSpec((1, window, LANE), lambda i, hp, j: (i, 0, hp))],
        out_shape=[jax.ShapeDtypeStruct((b, s, WIDTH), F32),
                   jax.ShapeDtypeStruct((b, s, WIDTH), F32),
                   jax.ShapeDtypeStruct((b, window, WIDTH), F32)],
        scratch_shapes=[pltpu.VMEM((ATT_ROWS, LANE), F32), pltpu.VMEM((ATT_ROWS, LANE), F32),
                        pltpu.VMEM((2, dil, BLK, LANE), BF16), pltpu.VMEM((2, dil, BLK, LANE), BF16)],
        compiler_params=_params(3),
        name=f"attn_prompt_g{g}",
    )(pv, pv, pv, cos[:, :LANE], sin[:, :LANE])
    return o.reshape(b * s, WIDTH), lse.reshape(b * s, WIDTH), kwin


def _rope_sample_body(p_ref, cos_ref, sin_ref, o_ref):
    cos = cos_ref[...]
    sin = sin_ref[...]
    for part in range(3 * N_GROUPS):
        sl = slice(part * WIDTH, (part + 1) * WIDTH)
        x = p_ref[:, sl]
        if part < N_GROUPS:
            o_ref[:, sl] = _rope(x, cos, sin) * (HEAD_DIM ** -0.5)
        elif part < 2 * N_GROUPS:
            o_ref[:, sl] = _rope(x, cos, sin)
        else:
            o_ref[:, sl] = x


def _rope_sample(p_att, cos, sin):
    m = p_att.shape[0]
    full = lambda a: pl.BlockSpec(a.shape, lambda i: (0, 0))
    return pl.pallas_call(
        _rope_sample_body,
        grid=(1,),
        in_specs=[full(p_att), full(cos), full(sin)],
        out_specs=full(p_att),
        out_shape=jax.ShapeDtypeStruct((m, C_ATT), F32),
        compiler_params=_params(1),
        name="rope_sample",
    )(p_att, cos, sin)


def _cache_roll_body(a_ref, b_ref, o_ref, *, rows, t):
    o_ref[0, :, 0:rows - t] = a_ref[0, :, t:rows]
    o_ref[0, :, rows - t:rows] = b_ref[0]


def _cache_roll(cache, t):
    depth, b, _, wb = cache.shape[:4]
    n = depth * b
    rows = min(wb, ROLL_ROWS)
    assert wb % rows == 0 and rows % t == 0
    flat = cache.reshape((n,) + cache.shape[2:])
    tile = cache.shape[4:]
    last = wb // t - 1
    out = pl.pallas_call(
        functools.partial(_cache_roll_body, rows=rows, t=t),
        grid=(n, wb // rows),
        in_specs=[pl.BlockSpec((1, 2, rows) + tile, lambda i, c: (i, 0, c, 0, 0)),
                  pl.BlockSpec((1, 2, t) + tile,
                               lambda i, c: (i, 0, jnp.minimum((c + 1) * (rows // t), last), 0, 0))],
        out_specs=pl.BlockSpec((1, 2, rows) + tile, lambda i, c: (i, 0, c, 0, 0)),
        out_shape=jax.ShapeDtypeStruct(flat.shape, F32),
        compiler_params=_params(2),
        name="cache_roll",
    )(flat, flat)
    return out.reshape(cache.shape)


def _attn_sample_body(q_ref, k_ref, v_ref, c_ref, _, o_ref, lse_ref, out_ref, sem, *,
                      layer, wb, dil, t):
    i = pl.program_id(0)
    copies = [
        pltpu.make_async_copy(k_ref.at[0, :, 0], out_ref.at[layer, i, 0, pl.ds(wb - t, t)], sem.at[0]),
        pltpu.make_async_copy(v_ref.at[0, :, 0], out_ref.at[layer, i, 1, pl.ds(wb - t, t)], sem.at[1]),
    ]
    for cp in copies:
        cp.start()
    rows = wb // dil
    for tq in range(t):
        r, i0 = tq % dil, tq // dil
        q = q_ref[0, tq, 0]
        kc = c_ref[0, 0, 0, i0:rows, r]
        vc = c_ref[0, 0, 1, i0:rows, r]
        new = [r + dil * m for m in range(i0 + 1)]
        s_c = jnp.sum(q[None] * kc, axis=-1, keepdims=True)
        s_n = [jnp.sum(q * k_ref[0, tn, 0], axis=-1, keepdims=True) for tn in new]
        mx = jnp.max(s_c, axis=0)
        for s in s_n:
            mx = jnp.maximum(mx, s)
        p_c = jnp.exp(s_c - mx[None])
        p_n = [jnp.exp(s - mx) for s in s_n]
        den = jnp.sum(p_c, axis=0)
        acc = jnp.sum(p_c * vc, axis=0)
        for p, tn in zip(p_n, new):
            den = den + p
            acc = acc + p * v_ref[0, tn, 0]
        o_ref[0, tq] = acc / den
        lse_ref[0, tq] = jnp.broadcast_to(mx + jnp.log(den), (N_HEADS, HEAD_DIM))
    for cp in copies:
        cp.wait()


def _attn_sample(p_nat, cache, stacked, layer, b, t, g):
    window, dil = ATT_GROUPS[g]
    depth, wb = cache.shape[0], cache.shape[3]
    assert wb == window and wb // dil == BLK and t <= wb
    ncls = min(dil, t)
    cls = cache.reshape(depth, b, 2, wb // dil, dil, N_HEADS, HEAD_DIM)
    blk = lambda part: pl.BlockSpec((1, t, 1, N_HEADS, HEAD_DIM),
                                    lambda i: (i, 0, part * N_GROUPS + g, 0, 0))
    cspec = pl.BlockSpec((1, 1, 2, wb // dil, ncls, N_HEADS, HEAD_DIM),
                         lambda i: (layer, i, 0, 0, 0, 0, 0))
    anyspec = pl.BlockSpec(memory_space=pl.ANY)
    ospec = pl.BlockSpec((1, t, N_HEADS, HEAD_DIM), lambda i: (i, 0, 0, 0))
    o, lse, stacked = pl.pallas_call(
        functools.partial(_attn_sample_body, layer=layer, wb=wb, dil=dil, t=t),
        grid=(b,),
        in_specs=[blk(0), blk(1), blk(2), cspec, anyspec],
        out_specs=[ospec, ospec, anyspec],
        out_shape=[jax.ShapeDtypeStruct((b, t, N_HEADS, HEAD_DIM), F32),
                   jax.ShapeDtypeStruct((b, t, N_HEADS, HEAD_DIM), F32),
                   jax.ShapeDtypeStruct(cache.shape, F32)],
        scratch_shapes=[pltpu.SemaphoreType.DMA((2,))],
        input_output_aliases={4: 2},
        compiler_params=_params(1),
        name=f"attn_sample_g{g}",
    )(p_nat, p_nat, p_nat, cls, stacked)
    return o.reshape(b * t, WIDTH), lse.reshape(b * t, WIDTH), stacked


def _merge_body(x_ref, pg_ref, y_ref, r_ref, k_ref, v_ref, g_ref, o0_ref, o1_ref, o2_ref,
                l0_ref, l1_ref, l2_ref, wa_ref, wb_ref, wo_ref, gng_ref, gnb_ref, rk_ref,
                lng_ref, lnb_ref, ones_ref, h_ref, *, alpha):
    ones_bd = ones_ref[...]
    y = y_ref[...]
    v = v_ref[...]
    mu = _head_sum(y, ones_bd) * (1.0 / HEAD_DIM)
    d = y - mu
    var = _head_sum(d * d, ones_bd) * (1.0 / HEAD_DIM)
    yn = d * lax.rsqrt(var + GN_EPS) * gng_ref[...] + gnb_ref[...]
    bonus = _head_sum(r_ref[...] * k_ref[...] * rk_ref[...], ones_bd) * v
    rw = (yn + bonus) * g_ref[...]
    l0, l1, l2 = l0_ref[...], l1_ref[...], l2_ref[...]
    mx = jnp.maximum(jnp.maximum(l0, l1), l2)
    e0, e1, e2 = jnp.exp(l0 - mx), jnp.exp(l1 - mx), jnp.exp(l2 - mx)
    att = (e0 * o0_ref[...] + e1 * o1_ref[...] + e2 * o2_ref[...]) / (e0 + e1 + e2)
    pg = pg_ref[...]
    merged = (_sigmoid(pg[:, :D_MODEL]) * _dot(rw.astype(BF16), wa_ref[...])
              + _sigmoid(pg[:, D_MODEL:]) * _dot(att.astype(BF16), wb_ref[...]))
    pre = alpha * x_ref[...] + _dot(merged.astype(BF16), wo_ref[...])
    h_ref[...] = _layer_norm(pre, lng_ref[...], lnb_ref[...])


def _merge(x, p_gate, y, r, k, v, g, outs, lses, lw, tm, alpha):
    m = x.shape[0]
    row = lambda c: pl.BlockSpec((tm, c), lambda i: (i, 0))
    const = lambda a: pl.BlockSpec(a.shape, lambda i: (0,) * a.ndim)
    consts = [lw["w_br_a"], lw["w_br_b"], lw["w_out"], lw["gn_g"], lw["gn_b"], lw["rk"],
              lw["ln1_g"], lw["ln1_b"], lw["ones_bd"]]
    return pl.pallas_call(
        functools.partial(_merge_body, alpha=alpha),
        grid=(m // tm,),
        in_specs=[row(D_MODEL), row(C_GATE)] + [row(WIDTH)] * 11 + [const(a) for a in consts],
        out_specs=row(D_MODEL),
        out_shape=jax.ShapeDtypeStruct((m, D_MODEL), F32),
        compiler_params=_params(1),
        name="merge_ln1",
    )(x, p_gate, y, r, k, v, g, *outs, *lses, *consts)


def _gelu(x):
    return 0.5 * x * (1.0 + lax.erf(x * (2.0 ** -0.5)))


def _ffn_body(*refs, seq, tm, nf, carry_mode, alpha):
    if carry_mode:
        (h_ref, wu_ref, wg_ref, wd_ref, cw_ref, cb_ref, lng_ref, lnb_ref, buf_ref,
         y_ref, cnew_ref, hb_ref, acc_ref, carry_ref) = refs
    else:
        (h_ref, wu_ref, wg_ref, wd_ref, cw_ref, cb_ref, lng_ref, lnb_ref, e1_ref, e2_ref,
         y_ref, u_ref, hb_ref, acc_ref) = refs
    i = pl.program_id(0)
    j = pl.program_id(1)

    @pl.when(j == 0)
    def _():
        hb_ref[...] = h_ref[...].astype(BF16)
        acc_ref[...] = jnp.zeros_like(acc_ref)

    hb = hb_ref[...]
    u = _dot(hb, wu_ref[...])
    gate = _dot(hb, wg_ref[...])
    row = lax.broadcasted_iota(jnp.int32, u.shape, 0)
    r1 = pltpu.roll(u, 1, 0)
    r2 = pltpu.roll(u, 2, 0)
    if carry_mode:
        slot = i % 2

        @pl.when(i % (seq // tm) == 0)
        def _():
            carry_ref[1 - slot, j, 6:8, :] = buf_ref[0]

        tail = carry_ref[1 - slot, j]
        u1 = jnp.where(row == 0, tail[7:8, :], r1)
        u2 = jnp.where(row == 0, tail[6:7, :], jnp.where(row == 1, tail[7:8, :], r2))
        carry_ref[slot, j] = u[tm - 8:tm, :]
        cnew_ref[0] = u[tm - 2:tm, :]
    else:
        u1 = jnp.where(row % seq == 0, e1_ref[...], r1)
        u2 = jnp.where(row % seq < 2, e2_ref[...], r2)
        u_ref[...] = u
    cw = cw_ref[...]
    conv = cb_ref[...] + cw[0:1, :] * u2 + cw[1:2, :] * u1 + cw[2:3, :] * u
    act = _gelu(conv) * gate
    acc_ref[...] += _dot(act.astype(BF16), wd_ref[...])

    @pl.when(j == nf - 1)
    def _():
        y_ref[...] = _layer_norm(alpha * h_ref[...] + acc_ref[...], lng_ref[...], lnb_ref[...])


def _conv_ffn(h, conv_buf, lw, seq, tm, carry_mode, alpha):
    m = h.shape[0]
    b = m // seq
    nf = D_FF // FF_CHUNK
    tf = FF_CHUNK
    const = lambda a: pl.BlockSpec(a.shape, lambda i, j: (0,) * a.ndim)
    in_specs = [pl.BlockSpec((tm, D_MODEL), lambda i, j: (i, 0)),
                pl.BlockSpec((D_MODEL, tf), lambda i, j: (0, j)),
                pl.BlockSpec((D_MODEL, tf), lambda i, j: (0, nf + j)),
                pl.BlockSpec((tf, D_MODEL), lambda i, j: (j, 0)),
                pl.BlockSpec((CONV_W, tf), lambda i, j: (0, j)),
                pl.BlockSpec((1, tf), lambda i, j: (0, j)),
                const(lw["ln2_g"]), const(lw["ln2_b"])]
    args = [h, lw["w_up"], lw["w_up"], lw["w_down"], lw["conv_w"], lw["conv_b"],
            lw["ln2_g"], lw["ln2_b"]]
    scratch = [pltpu.VMEM((tm, D_MODEL), BF16), pltpu.VMEM((tm, D_MODEL), F32)]
    y_spec = pl.BlockSpec((tm, D_MODEL), lambda i, j: (i, 0))
    y_shape = jax.ShapeDtypeStruct((m, D_MODEL), F32)
    if carry_mode:
        tps = seq // tm
        in_specs.append(pl.BlockSpec((1, CONV_W - 1, tf), lambda i, j: (i // tps, 0, j)))
        args.append(conv_buf)
        out_specs = [y_spec, pl.BlockSpec((1, CONV_W - 1, tf), lambda i, j: (i, 0, j))]
        out_shape = [y_shape, jax.ShapeDtypeStruct((m // tm, CONV_W - 1, D_FF), F32)]
        scratch.append(pltpu.VMEM((2, nf, 8, tf), F32))
    else:
        zeros = jnp.zeros((b, seq, D_FF), F32)
        e1 = zeros.at[:, 0].set(conv_buf[:, 1]).reshape(m, D_FF)
        e2 = zeros.at[:, 0].set(conv_buf[:, 0]).at[:, 1].set(conv_buf[:, 1]).reshape(m, D_FF)
        tile = pl.BlockSpec((tm, tf), lambda i, j: (i, j))
        in_specs += [tile, tile]
        args += [e1, e2]
        out_specs = [y_spec, tile]
        out_shape = [y_shape, jax.ShapeDtypeStruct((m, D_FF), F32)]
    y, aux = pl.pallas_call(
        functools.partial(_ffn_body, seq=seq, tm=tm, nf=nf, carry_mode=carry_mode, alpha=alpha),
        grid=(m // tm, nf),
        in_specs=in_specs,
        out_specs=out_specs,
        out_shape=out_shape,
        scratch_shapes=scratch,
        compiler_params=_params(2),
        name="conv_ffn_ln2",
    )(*args)
    if carry_mode:
        return y, aux[seq // tm - 1::seq // tm]
    return y, aux.reshape(b, seq, D_FF)[:, seq - (CONV_W - 1):]


def _rope_tables(pos):
    half = HEAD_DIM // 2
    inv = ROPE_THETA ** (-jnp.arange(half, dtype=F32) / half)
    ang = pos.astype(F32)[:, None] * inv[None, :]
    cos, sin = jnp.cos(ang), jnp.sin(ang)
    cos = jnp.tile(jnp.concatenate([cos, cos], axis=-1), (1, N_HEADS))
    sin = jnp.tile(jnp.concatenate([-sin, sin], axis=-1), (1, N_HEADS))
    return cos, sin


def _pad_cols(a, n):
    return jnp.pad(a, ((0, 0), (0, n - a.shape[1])))


def _layer_weights(l, w):
    row = lambda a: a.reshape(1, -1)
    z = lambda r, c: jnp.zeros((r, c), F32)
    w_in = w["w_in"][l]
    if l == 0:
        vres, mu_v = z(D_MODEL, D_MV_LORA), z(1, D_MV_LORA)
        v0, v2 = z(1, WIDTH), z(D_MV_LORA, WIDTH)
    else:
        vres, mu_v = w["w_in_vres"][l - 1], row(w["mu_vres"][l - 1])
        v0, v2 = row(w["rw_v0"][l - 1]), w["rw_v2"][l - 1]
    w_rw = _pad_cols(jnp.concatenate([w_in[:, C_ATT + C_GATE:], vres], axis=1), C_RW_PAD)
    mu = _pad_cols(jnp.concatenate([row(w["mu_rw"][l]), mu_v], axis=1), C_RW_PAD)
    w2a2 = jnp.concatenate([
        jnp.concatenate([w["rw_w2"][l], z(D_DECAY_LORA, WIDTH)], axis=1),
        jnp.concatenate([z(D_AAA_LORA, WIDTH), w["rw_a2"][l]], axis=1)], axis=0)
    pad_rows = 256 - D_GATE_LORA - D_MV_LORA
    g2v2 = jnp.concatenate([
        jnp.concatenate([w["rw_g2"][l], z(D_GATE_LORA, WIDTH)], axis=1),
        jnp.concatenate([z(D_MV_LORA, WIDTH), v2], axis=1),
        z(pad_rows, 2 * WIDTH)], axis=0)
    head = jnp.arange(WIDTH) // HEAD_DIM
    return {
        "w_att": w_in[:, :C_ATT].astype(BF16),
        "w_gate": w_in[:, C_ATT:C_ATT + C_GATE].astype(BF16),
        "w_rw": w_rw.astype(BF16),
        "mu": mu, "w0": row(w["rw_w0"][l]), "a0": row(w["rw_a0"][l]), "v0": v0,
        "kk": row(w["rw_kk"][l]), "ka": row(w["rw_ka"][l]),
        "w2a2": w2a2.astype(BF16), "g2v2": g2v2.astype(BF16),
        "ones_bd": (head[:, None] == head[None, :]).astype(BF16),
        "is_l0": jnp.full((1, WIDTH), 1.0 if l == 0 else 0.0, F32),
        "gn_g": row(w["rw_gn_g"][l]), "gn_b": row(w["rw_gn_b"][l]), "rk": row(w["rw_rk"][l]),
        "w_br_a": w["w_br_a"][l].astype(BF16), "w_br_b": w["w_br_b"][l].astype(BF16),
        "w_out": w["w_out"][l].astype(BF16),
        "ln1_g": row(w["ln1_g"][l]), "ln1_b": row(w["ln1_b"][l]),
        "w_up": w["ffn_w_up"][l].astype(BF16), "w_down": w["ffn_w_down"][l].astype(BF16),
        "conv_w": w["ffn_conv_w"][l], "conv_b": row(w["ffn_conv_b"][l]),
        "ln2_g": row(w["ln2_g"][l]), "ln2_b": row(w["ln2_b"][l]),
    }


def _trunk_layer(x, b, t, x_prev, wkv0, caches, conv_buf, v_first, lw, cos, sin, alpha, layer=0,
                 stacked=None):
    m = b * t
    prompt = caches is None
    tm = min(m, 1024 if prompt else 256)
    p_att = _mm(x, lw["w_att"], tm, WIDTH)
    p_gate = _mm(x, lw["w_gate"], tm, WIDTH)
    p_rw = _mm(x, lw["w_rw"], tm, C_RW_PAD // 3)
    bp = -(-b // 8) * 8
    prev0 = _mm(jnp.pad(x_prev, ((0, bp - b), (0, 0))), lw["w_rw"], bp, C_RW_PAD // 3)[:b]

    if not prompt:
        p_nat = _rope_sample(p_att, jnp.tile(cos, (b, 1)), jnp.tile(sin, (b, 1)))
        p_nat = p_nat.reshape(b, t, 3 * N_GROUPS, N_HEADS, HEAD_DIM)
    outs, lses, wins = [], [], []
    for g, (window, dil) in enumerate(ATT_GROUPS):
        if prompt:
            o, lse, kwin = _attn_prompt(p_att, cos, sin, b, t, g)
            lo = (2 * N_GROUPS + g) * WIDTH
            vwin = p_att.reshape(b, t, C_ATT)[:, t - window:, lo:lo + WIDTH]
            win = jnp.stack([kwin, vwin], axis=1).reshape(b, 2, window, N_HEADS, HEAD_DIM)
        else:
            o, lse, win = _attn_sample(p_nat, caches[g], stacked[g], layer, b, t, g)
        outs.append(o)
        lses.append(lse)
        wins.append(win)

    carry_mode = t >= 512
    tm_rw = 512 if carry_mode else m
    if carry_mode:
        prev_in = prev0.reshape(b, 1, C_RW_PAD)
    else:
        prev_in = jnp.repeat(prev0, t, axis=0)
    vf_in = jnp.zeros((m, WIDTH), F32) if v_first is None else v_first
    r, lgw, k, v, kk, a, g_out = _time_mix(p_rw, prev_in, vf_in, lw, t, tm_rw, carry_mode)
    if v_first is None:
        v_first = v

    tp = -(-t // WKV_CHUNK) * WKV_CHUNK
    tok = lambda a_: jnp.pad(a_.reshape(b, t, WIDTH), ((0, 0), (0, tp - t), (0, 0)))
    y, wkv_new = _wkv(tok(r), tok(lgw), tok(k), tok(v), tok(kk), tok(a), wkv0)
    y = y[:, :t].reshape(m, WIDTH)

    tm_mg = min(m, 256)
    h = _merge(x, p_gate, y, r, k, v, g_out, outs, lses, lw, tm_mg, alpha)
    y_out, conv_new = _conv_ffn(h, conv_buf, lw, t, 1024 if carry_mode else m, carry_mode, alpha)
    shift = x.reshape(b, t, D_MODEL)[:, -1]
    return y_out, wins, wkv_new, shift, conv_new, v_first


def kernel(x_prompt, x_sample, cache_win128, cache_win512, cache_win2048, state_wkv, state_shift, state_ffn_conv, w_in, w_in_vres, mu_rw, mu_vres, rw_w0, rw_w2, rw_a0, rw_a2, rw_g2, rw_v0, rw_v2, rw_kk, rw_ka, rw_rk, rw_gn_g, rw_gn_b, w_br_a, w_br_b, w_out, ln1_g, ln1_b, ffn_w_up, ffn_conv_w, ffn_conv_b, ffn_w_down, ln2_g, ln2_b):
    w = dict(w_in=w_in, w_in_vres=w_in_vres, mu_rw=mu_rw, mu_vres=mu_vres, rw_w0=rw_w0, rw_w2=rw_w2,
             rw_a0=rw_a0, rw_a2=rw_a2, rw_g2=rw_g2, rw_v0=rw_v0, rw_v2=rw_v2, rw_kk=rw_kk,
             rw_ka=rw_ka, rw_rk=rw_rk, rw_gn_g=rw_gn_g, rw_gn_b=rw_gn_b, w_br_a=w_br_a,
             w_br_b=w_br_b, w_out=w_out, ln1_g=ln1_g, ln1_b=ln1_b, ffn_w_up=ffn_w_up,
             ffn_conv_w=ffn_conv_w, ffn_conv_b=ffn_conv_b, ffn_w_down=ffn_w_down,
             ln2_g=ln2_g, ln2_b=ln2_b)
    caches = (cache_win128, cache_win512, cache_win2048)
    depth = w_in.shape[0]
    alpha = ALPHA
    bp, sp, _ = x_prompt.shape
    bs, ts, _ = x_sample.shape
    cos_p, sin_p = _rope_tables(jnp.arange(sp, dtype=jnp.int32))
    cos_s, sin_s = _rope_tables(PAST_LEN + jnp.arange(ts, dtype=jnp.int32))
    xp = x_prompt.reshape(bp * sp, D_MODEL)
    xs = x_sample.reshape(bs * ts, D_MODEL)
    vf_p = vf_s = None
    win_p = [[] for _ in ATT_GROUPS]
    win_s = [_cache_roll(c, ts) for c in caches]
    wkv_p, wkv_s, sh_p, sh_s, cv_p, cv_s = [], [], [], [], [], []
    for l in range(depth):
        lw = _layer_weights(l, w)
        xp, nw, s_new, sh, cv, vf_p = _trunk_layer(
            xp, bp, sp, jnp.zeros((bp, D_MODEL), F32),
            jnp.zeros((bp, N_HEADS, HEAD_DIM, HEAD_DIM), F32), None,
            jnp.zeros((bp, CONV_W - 1, D_FF), F32), vf_p, lw, cos_p, sin_p, alpha)
        for g in range(N_GROUPS):
            win_p[g].append(nw[g])
        wkv_p.append(s_new)
        sh_p.append(sh)
        cv_p.append(cv)
        xs, win_s, s_new, sh, cv, vf_s = _trunk_layer(
            xs, bs, ts, state_shift[l], state_wkv[l], caches,
            state_ffn_conv[l], vf_s, lw, cos_s, sin_s, alpha, l, win_s)
        wkv_s.append(s_new)
        sh_s.append(sh)
        cv_s.append(cv)
    return (xp.reshape(bp, sp, D_MODEL), xs.reshape(bs, ts, D_MODEL),
            jnp.stack(win_p[0]), win_s[0], jnp.stack(win_p[1]), win_s[1],
            jnp.stack(win_p[2]), win_s[2], jnp.stack(wkv_p), jnp.stack(wkv_s),
            jnp.stack(sh_p), jnp.stack(sh_s), jnp.stack(cv_p), jnp.stack(cv_s))
```

```python
import functools

import jax
import jax.numpy as jnp
from jax import lax
from jax.experimental import pallas as pl
from jax.experimental.pallas import tpu as pltpu

F32 = jnp.float32
BF16 = jnp.bfloat16

D_MODEL = 1024
HEAD_DIM = 64
N_HEADS = 8
WIDTH = N_HEADS * HEAD_DIM
ATT_GROUPS = ((128, 1), (512, 4), (2048, 16))
N_GROUPS = len(ATT_GROUPS)
BLK = 128
ROPE_THETA = 10000.0
D_DECAY_LORA = 64
D_AAA_LORA = 64
D_GATE_LORA = 160
D_MV_LORA = 32
D_FF = 2816
CONV_W = 3
LN_EPS = 1e-5
GN_EPS = 64e-5
DEPTH = 4
ALPHA = (2.0 * DEPTH) ** 0.25
PAST_LEN = 8192
C_ATT = 3 * N_GROUPS * WIDTH
C_GATE = 2 * D_MODEL
C_RW = 3 * WIDTH + D_DECAY_LORA + D_AAA_LORA + D_GATE_LORA
C_RW_PAD = 1920
LORA_WA = 3 * WIDTH
LORA_GV = LORA_WA + 128
FF_CHUNK = 256
WKV_CHUNK = 64
ATT_ROWS = 2048
LANE = 128
VMEM_LIMIT = 56 * 1024 * 1024
NEG_INF = float("-inf")


def _params(n_axes):
    return pltpu.CompilerParams(
        dimension_semantics=("arbitrary",) * n_axes, vmem_limit_bytes=VMEM_LIMIT)


def _dot(a, b):
    return jnp.dot(a, b, preferred_element_type=F32)


NN = ((1,), (0,))
NT = ((1,), (1,))
TN = ((0,), (0,))


def _dg(a, b, dims):
    return lax.dot_general(a, b, (dims, ((), ())), preferred_element_type=F32)


def _dot_nt(a, b):
    return _dg(a, b, NT)


def _split(x):
    hi = x.astype(BF16)
    return hi, (x - hi.astype(F32)).astype(BF16)


def _dot_x1(a, b, dims=NN):
    return _dg(a.astype(BF16), b.astype(BF16), dims)


def _head_sum(x, ones_bd):
    hi, lo = _split(x)
    return _dot(hi, ones_bd) + _dot(lo, ones_bd)


def _layer_norm(x, g, b):
    mu = jnp.mean(x, axis=-1, keepdims=True)
    d = x - mu
    var = jnp.mean(d * d, axis=-1, keepdims=True)
    return d * lax.rsqrt(var + LN_EPS) * g + b


def _sigmoid(x):
    return 1.0 / (1.0 + jnp.exp(-x))


def _rope(t, cos, sin):
    half = HEAD_DIM // 2
    outs = []
    for c in range(t.shape[-1] // LANE):
        tc = t[:, c * LANE:(c + 1) * LANE]
        lane = lax.broadcasted_iota(jnp.int32, tc.shape, 1)
        fwd = pltpu.roll(tc, LANE - half, 1)
        bwd = pltpu.roll(tc, half, 1)
        outs.append(jnp.where((lane & (HEAD_DIM - 1)) < half, fwd, bwd))
    partner = jnp.concatenate(outs, axis=-1)
    return t * cos + partner * sin


def _mm_body(x_ref, w_ref, o_ref, xb_ref):
    @pl.when(pl.program_id(1) == 0)
    def _():
        xb_ref[...] = x_ref[...].astype(BF16)

    o_ref[...] = _dot(xb_ref[...], w_ref[...])


def _mm(x, w, tm, tn):
    m, k = x.shape
    n = w.shape[1]
    return pl.pallas_call(
        _mm_body,
        grid=(m // tm, n // tn),
        in_specs=[pl.BlockSpec((tm, k), lambda i, j: (i, 0)),
                  pl.BlockSpec((k, tn), lambda i, j: (0, j))],
        out_specs=pl.BlockSpec((tm, tn), lambda i, j: (i, j)),
        out_shape=jax.ShapeDtypeStruct((m, n), F32),
        scratch_shapes=[pltpu.VMEM((tm, k), BF16)],
        compiler_params=_params(2),
        name="proj_mm",
    )(x, w)


def _tm_body(*refs, seq, tm, carry_mode):
    if carry_mode:
        (p_ref, prev0_ref, mu_ref, w0_ref, a0_ref, v0_ref, kkp_ref, kap_ref, w2a2_ref, g2v2_ref,
         ones_ref, vf_ref, l0_ref,
         r_out, lw_out, k_out, v_out, kk_out, a_out, g_out, carry_ref) = refs
    else:
        (p_ref, prev0_ref, mu_ref, w0_ref, a0_ref, v0_ref, kkp_ref, kap_ref, w2a2_ref, g2v2_ref,
         ones_ref, vf_ref, l0_ref,
         r_out, lw_out, k_out, v_out, kk_out, a_out, g_out) = refs
    p = p_ref[...]
    row = lax.broadcasted_iota(jnp.int32, p.shape, 0)
    rolled = pltpu.roll(p, 1, 0)
    if carry_mode:
        slot = pl.program_id(0) % 2

        @pl.when(pl.program_id(0) % (seq // tm) == 0)
        def _():
            carry_ref[1 - slot, 0:1, :] = prev0_ref[0]

        p_prev = jnp.where(row == 0, carry_ref[1 - slot, 0:1, :], rolled)
        carry_ref[slot, 0:1, :] = p[tm - 1:tm, :]
    else:
        p_prev = jnp.where(row % seq == 0, prev0_ref[...], rolled)
    z = p + (p_prev - p) * mu_ref[...]
    r = z[:, :WIDTH]
    k = z[:, WIDTH:2 * WIDTH]
    v = z[:, 2 * WIDTH:3 * WIDTH]
    wa = z[:, LORA_WA:LORA_WA + 128]
    lane = lax.broadcasted_iota(jnp.int32, wa.shape, 1)
    wa = jnp.where(lane < D_DECAY_LORA, jnp.tanh(wa), wa)
    wa_out = _dot(wa.astype(BF16), w2a2_ref[...])
    gv = z[:, LORA_GV:LORA_GV + 256]
    lane = lax.broadcasted_iota(jnp.int32, gv.shape, 1)
    gv = jnp.where(lane < D_GATE_LORA, _sigmoid(gv), gv)
    gv_out = _dot(gv.astype(BF16), g2v2_ref[...])
    x = -(w0_ref[...] + wa_out[:, :WIDTH])
    softplus = jnp.maximum(x, 0.0) + jnp.log1p(jnp.exp(-jnp.abs(x)))
    lw_out[...] = -jnp.exp(-softplus - 0.5)
    a = _sigmoid(a0_ref[...] + wa_out[:, WIDTH:])
    g_out[...] = gv_out[:, :WIDTH]
    vf = jnp.where(l0_ref[...] > 0.5, v, vf_ref[...])
    v = v + (vf - v) * _sigmoid(v0_ref[...] + gv_out[:, WIDTH:])
    kk = k * kkp_ref[...]
    norm = jnp.sqrt(_head_sum(kk * kk, ones_ref[...]))
    kk_out[...] = kk / jnp.maximum(norm, 1e-12)
    k_out[...] = k * (1.0 + (a - 1.0) * kap_ref[...])
    r_out[...] = r
    v_out[...] = v
    a_out[...] = a


def _time_mix(p_rw, prev0, vfirst, lw, seq, tm, carry_mode):
    m = p_rw.shape[0]
    row = lambda c: pl.BlockSpec((tm, c), lambda i: (i, 0))
    const = lambda a: pl.BlockSpec(a.shape, lambda i: (0,) * a.ndim)
    if carry_mode:
        prev_spec = pl.BlockSpec((1, 1, C_RW_PAD), lambda i: (i // (seq // tm), 0, 0))
        scratch = [pltpu.VMEM((2, 8, C_RW_PAD), F32)]
    else:
        prev_spec = row(C_RW_PAD)
        scratch = []
    consts = [lw["mu"], lw["w0"], lw["a0"], lw["v0"], lw["kk"], lw["ka"], lw["w2a2"], lw["g2v2"],
              lw["ones_bd"]]
    out = jax.ShapeDtypeStruct((m, WIDTH), F32)
    return pl.pallas_call(
        functools.partial(_tm_body, seq=seq, tm=tm, carry_mode=carry_mode),
        grid=(m // tm,),
        in_specs=[row(C_RW_PAD), prev_spec] + [const(a) for a in consts]
        + [row(WIDTH), const(lw["is_l0"])],
        out_specs=[row(WIDTH)] * 7,
        out_shape=[out] * 7,
        scratch_shapes=scratch,
        compiler_params=_params(1),
        name="time_mix",
    )(p_rw, prev0, *consts, vfirst, lw["is_l0"])


def _wkv_body(r_ref, lw_ref, k_ref, v_ref, kk_ref, a_ref, s0_ref, y_ref, sout_ref, s_ref, *, nc):
    c = pl.program_id(1)
    n = WKV_CHUNK

    @pl.when(c == 0)
    def _():
        s_ref[...] = s0_ref[0]

    lw = lw_ref[0]
    row = lax.broadcasted_iota(jnp.int32, (n, n), 0)
    col = lax.broadcasted_iota(jnp.int32, (n, n), 1)
    incl = col <= row
    strict = col < row
    eye = (col == row).astype(F32)
    row2 = lax.broadcasted_iota(jnp.int32, (n, 2 * n), 0)
    col2 = lax.broadcasted_iota(jnp.int32, (n, 2 * n), 1)
    incl_cat = (col2 & (n - 1)) <= row2
    tri = incl.astype(BF16)
    lw_hi = lw.astype(BF16)
    lw_mid, lw_lo = _split(lw - lw_hi.astype(F32))
    cum =_dot(tri, lw_hi) + _dot(tri, lw_mid) + _dot(tri, lw_lo)
    cum_end = cum[n - 1:n, :]
    kk = kk_ref[0]
    k = k_ref[0]
    kka = kk * a_ref[0]
    e_neg = jnp.exp(-cum)
    e_end = jnp.exp(cum_end - cum)
    a_t = -kk * jnp.exp(cum - lw)
    b_t = kka * e_neg
    k_t = k * e_neg
    r_t = r_ref[0] * jnp.exp(cum)
    b_h = kka * e_end
    k_h = k * e_end
    g_end = jnp.exp(cum_end)
    v = v_ref[0]
    heads = range(N_HEADS)
    hs = [slice(h * HEAD_DIM, (h + 1) * HEAD_DIM) for h in heads]
    s = [s_ref[h] for h in heads]
    ar = [jnp.concatenate([a_t[:, sl], r_t[:, sl]], axis=0) for sl in hs]
    bk = [jnp.concatenate([b_t[:, sl], k_t[:, sl]], axis=0) for sl in hs]
    gram = [_dot_x1(ar[h], bk[h], NT) for h in heads]
    on_s = [_dot_x1(ar[h], s[h], NT) for h in heads]
    l_ab = [jnp.where(strict, g[:n, :n], 0.0) for g in gram]
    l_ak = [jnp.where(strict, g[:n, n:], 0.0) for g in gram]
    m_cat = [jnp.where(incl_cat, g[n:, :], 0.0) for g in gram]
    rhs = [on_s[h][:n] + _dot_x1(l_ak[h], v[:, hs[h]]) for h in heads]
    acc = [eye + l for l in l_ab]
    pw = [_dot_x1(l, l) for l in l_ab]
    span = 2
    while span * 2 < n:
        both = [_dot_x1(jnp.concatenate([acc[h], pw[h]], axis=0), pw[h]) for h in heads]
        acc = [acc[h] + both[h][:n] for h in heads]
        pw = [both[h][n:] for h in heads]
        span *= 2
    acc = [acc[h] + _dot_x1(acc[h], pw[h]) for h in heads]
    u = [_dot_x1(acc[h], rhs[h]) for h in heads]
    uv = [jnp.concatenate([u[h], v[:, hs[h]]], axis=0) for h in heads]
    for h in heads:
        y_ref[0, :, hs[h]] = on_s[h][n:] + _dot_x1(m_cat[h], uv[h])
    for h in heads:
        bk_end = jnp.concatenate([b_h[:, hs[h]], k_h[:, hs[h]]], axis=0)
        s_ref[h] = s[h] * g_end[:, hs[h]] + _dot_x1(uv[h], bk_end, TN)

    @pl.when(c == nc - 1)
    def _():
        sout_ref[0] = s_ref[...]


def _wkv(r, lw, k, v, kk, a, s0):
    b, t, _ = r.shape
    nc = t // WKV_CHUNK
    tok = pl.BlockSpec((1, WKV_CHUNK, WIDTH), lambda i, c: (i, c, 0))
    st = pl.BlockSpec((1, N_HEADS, HEAD_DIM, HEAD_DIM), lambda i, c: (i, 0, 0, 0))
    return pl.pallas_call(
        functools.partial(_wkv_body, nc=nc),
        grid=(b, nc),
        in_specs=[tok] * 6 + [st],
        out_specs=[tok, st],
        out_shape=[jax.ShapeDtypeStruct((b, t, WIDTH), F32),
                   jax.ShapeDtypeStruct((b, N_HEADS, HEAD_DIM, HEAD_DIM), F32)],
        scratch_shapes=[pltpu.VMEM((N_HEADS, HEAD_DIM, HEAD_DIM), F32)],
        compiler_params=_params(2),
        name="wkv_chunked",
    )(r, lw, k, v, kk, a, s0)


def _attn_prompt_body(q_ref, k_ref, v_ref, cos_ref, sin_ref, o_ref, lse_ref, kwin_ref,
                      qs_ref, ks_ref, kc_ref, vc_ref, *, nblk, dil, window):
    j = pl.program_id(2)
    slot = j % 2
    cos = cos_ref[...]
    sin = sin_ref[...]
    qs_ref[...] = _rope(q_ref[0], cos, sin) * (HEAD_DIM ** -0.5)
    k_rot = _rope(k_ref[0], cos, sin)
    ks_ref[...] = k_rot

    @pl.when(j == nblk - 1)
    def _():
        kwin_ref[0] = k_rot[ATT_ROWS - window:, :]

    @pl.when(j == 0)
    def _():
        kc_ref[1] = jnp.zeros((dil, BLK, LANE), BF16)
        vc_ref[1] = jnp.zeros((dil, BLK, LANE), BF16)

    lane = lax.broadcasted_iota(jnp.int32, (BLK, LANE), 1)
    first = lane < HEAD_DIM
    qi = lax.broadcasted_iota(jnp.int32, (BLK, BLK), 0)
    kj = lax.broadcasted_iota(jnp.int32, (BLK, BLK), 1)
    prev_ok = kj >= qi
    prev_ok_first = (kj - jnp.where(j > 0, 0, BLK)) >= qi
    cur_ok = kj <= qi
    pair = range(2)
    for r in range(dil):
        kp = kc_ref[1 - slot, r]
        vp = vc_ref[1 - slot, r]
        for m in range(ATT_ROWS // (BLK * dil)):
            start = r + dil * BLK * m
            rows = pl.ds(start, BLK, stride=dil) if dil > 1 else pl.ds(start, BLK)
            qf = qs_ref[rows, :]
            kb = ks_ref[rows, :].astype(BF16)
            vb = v_ref[0, rows, :].astype(BF16)
            q2 = [jnp.where(first, qf, 0.0).astype(BF16), jnp.where(first, 0.0, qf).astype(BF16)]
            ok = prev_ok_first if m == 0 else prev_ok
            s_prev = [jnp.where(ok, _dot_nt(q2[h], kp), NEG_INF) for h in pair]
            s_cur = [jnp.where(cur_ok, _dot_nt(q2[h], kb), NEG_INF) for h in pair]
            mx = [jnp.maximum(jnp.max(s_prev[h], axis=-1, keepdims=True),
                              jnp.max(s_cur[h], axis=-1, keepdims=True)) for h in pair]
            p_prev = [jnp.exp(s_prev[h] - mx[h]) for h in pair]
            p_cur = [jnp.exp(s_cur[h] - mx[h]) for h in pair]
            den = [jnp.sum(p_prev[h], axis=-1, keepdims=True)
                   + jnp.sum(p_cur[h], axis=-1, keepdims=True) for h in pair]
            acc = [_dot(p_prev[h].astype(BF16), vp) + _dot(p_cur[h].astype(BF16), vb) for h in pair]
            lse = [mx[h] + jnp.log(den[h]) for h in pair]
            o_ref[0, rows, :] = jnp.where(first, acc[0] / den[0], acc[1] / den[1])
            lse_ref[0, rows, :] = jnp.where(first, lse[0], lse[1])
            kp, vp = kb, vb
        kc_ref[slot, r] = kp
        vc_ref[slot, r] = vp


def _attn_prompt(p_att, cos, sin, b, s, g):
    window, dil = ATT_GROUPS[g]
    assert window // dil == BLK and window <= ATT_ROWS and s % ATT_ROWS == 0
    nblk = s // ATT_ROWS
    pairs = WIDTH // LANE
    pv = p_att.reshape(b, s, C_ATT)
    blk = lambda part: pl.BlockSpec(
        (1, ATT_ROWS, LANE), lambda i, hp, j: (i, j, (part * N_GROUPS + g) * pairs + hp))
    tspec = pl.BlockSpec((ATT_ROWS, LANE), lambda i, hp, j: (j, 0))
    ospec = pl.BlockSpec((1, ATT_ROWS, LANE), lambda i, hp, j: (i, j, hp))
    o, lse, kwin = pl.pallas_call(
        functools.partial(_attn_prompt_body, nblk=nblk, dil=dil, window=window),
        grid=(b, pairs, nblk),
        in_specs=[blk(0), blk(1), blk(2), tspec, tspec],
        out_specs=[ospec, ospec, pl.BlockSpec((1, window, LANE), lambda i, hp, j: (i, 0, hp))],
        out_shape=[jax.ShapeDtypeStruct((b, s, WIDTH), F32),
                   jax.ShapeDtypeStruct((b, s, WIDTH), F32),
                   jax.ShapeDtypeStruct((b, window, WIDTH), F32)],
        scratch_shapes=[pltpu.VMEM((ATT_ROWS, LANE), F32), pltpu.VMEM((ATT_ROWS, LANE), F32),
                        pltpu.VMEM((2, dil, BLK, LANE), BF16), pltpu.VMEM((2, dil, BLK, LANE), BF16)],
        compiler_params=_params(3),
        name=f"attn_prompt_g{g}",
    )(pv, pv, pv, cos[:, :LANE], sin[:, :LANE])
    return o.reshape(b * s, WIDTH), lse.reshape(b * s, WIDTH), kwin


def _rope_sample_body(p_ref, cos_ref, sin_ref, o_ref):
    cos = cos_ref[...]
    sin = sin_ref[...]
    for part in range(3 * N_GROUPS):
        sl = slice(part * WIDTH, (part + 1) * WIDTH)
        x = p_ref[:, sl]
        if part < N_GROUPS:
            o_ref[:, sl] = _rope(x, cos, sin) * (HEAD_DIM ** -0.5)
        elif part < 2 * N_GROUPS:
            o_ref[:, sl] = _rope(x, cos, sin)
        else:
            o_ref[:, sl] = x


def _rope_sample(p_att, cos, sin):
    m = p_att.shape[0]
    full = lambda a: pl.BlockSpec(a.shape, lambda i: (0, 0))
    return pl.pallas_call(
        _rope_sample_body,
        grid=(1,),
        in_specs=[full(p_att), full(cos), full(sin)],
        out_specs=full(p_att),
        out_shape=jax.ShapeDtypeStruct((m, C_ATT), F32),
        compiler_params=_params(1),
        name="rope_sample",
    )(p_att, cos, sin)


def _cache_roll_body(c_ref, o_ref, *, wb, t):
    for kv in range(2):
        for h in range(N_HEADS):
            x = c_ref[0, kv, h]
            o_ref[0, kv, h] = jnp.concatenate([x[:, t:], x[:, :t]], axis=1)


def _cache_roll(cache_t, t):
    depth, b = cache_t.shape[:2]
    mat = cache_t.shape[2:]
    wb = mat[-1]
    n = depth * b
    flat = cache_t.reshape((n,) + mat)
    spec = pl.BlockSpec((1,) + mat, lambda i: (i, 0, 0, 0, 0))
    out = pl.pallas_call(
        functools.partial(_cache_roll_body, wb=wb, t=t),
        grid=(n,),
        in_specs=[spec],
        out_specs=spec,
        out_shape=jax.ShapeDtypeStruct(flat.shape, F32),
        compiler_params=_params(1),
        name="cache_roll",
    )(flat)
    return out.reshape(cache_t.shape)


Q_PAD = 16


def _attn_sample_body(q_ref, kn_ref, vn_ref, c_ref, tail_ref, o_ref, lse_ref, tail_out, *,
                      wb, window, dil, t):
    heads = range(N_HEADS)
    tail_out[...] = tail_ref[...]
    tq = lax.broadcasted_iota(jnp.int32, (Q_PAD, wb), 0)
    w = lax.broadcasted_iota(jnp.int32, (Q_PAD, wb), 1)
    dist = wb + tq - w
    cache_ok = ((dist & (dil - 1)) == 0) & (dist <= window)
    tq = lax.broadcasted_iota(jnp.int32, (Q_PAD, LANE), 0)
    n = lax.broadcasted_iota(jnp.int32, (Q_PAD, LANE), 1)
    dist = tq - n
    new_ok = (n < t) & (dist >= 0) & ((dist & (dil - 1)) == 0)
    zrow = jnp.zeros((Q_PAD - t, HEAD_DIM), F32)
    zcol = jnp.zeros((HEAD_DIM, LANE - t), F32)
    q = [jnp.concatenate([q_ref[0, 0, h], zrow], axis=0).astype(BF16) for h in heads]
    kn = [jnp.concatenate([kn_ref[0, 0, h], zcol], axis=1).astype(BF16) for h in heads]
    vn = [jnp.concatenate([vn_ref[0, 0, h], zcol], axis=1).astype(BF16) for h in heads]
    s_c = [jnp.where(cache_ok, _dot(q[h], c_ref[0, 0, 0, h].astype(BF16)), NEG_INF) for h in heads]
    s_n = [jnp.where(new_ok, _dot(q[h], kn[h]), NEG_INF) for h in heads]
    mx = [jnp.maximum(jnp.max(s_c[h], axis=-1, keepdims=True),
                      jnp.max(s_n[h], axis=-1, keepdims=True)) for h in heads]
    p_c = [jnp.exp(s_c[h] - mx[h]) for h in heads]
    p_n = [jnp.exp(s_n[h] - mx[h]) for h in heads]
    den = [jnp.sum(p_c[h], axis=-1, keepdims=True) + jnp.sum(p_n[h], axis=-1, keepdims=True)
           for h in heads]
    acc = [_dot_nt(p_c[h].astype(BF16), c_ref[0, 0, 1, h].astype(BF16))
           + _dot_nt(p_n[h].astype(BF16), vn[h]) for h in heads]
    for h in heads:
        o_ref[0, h] = (acc[h] / den[h])[0:t]
        lse_ref[0, h] = jnp.broadcast_to((mx[h] + jnp.log(den[h]))[0:t], (t, HEAD_DIM))
        tail_out[0, 0, 0, h, :, LANE - t:LANE] = kn_ref[0, 0, h]
        tail_out[0, 0, 1, h, :, LANE - t:LANE] = vn_ref[0, 0, h]


def _attn_sample(q_hm, kv_t, cache_t, stacked, layer, b, t, g):
    window, dil = ATT_GROUPS[g]
    wb = cache_t.shape[-1]
    assert wb == window and wb % LANE == 0 and t <= Q_PAD
    mat = (N_HEADS, HEAD_DIM)
    qspec = pl.BlockSpec((1, 1, N_HEADS, t, HEAD_DIM), lambda i: (i, g, 0, 0, 0))
    nspec = lambda part: pl.BlockSpec((1, 1) + mat + (t,), lambda i: (i, part * N_GROUPS + g, 0, 0, 0))
    cspec = pl.BlockSpec((1, 1, 2) + mat + (wb,), lambda i: (layer, i, 0, 0, 0, 0))
    tspec = pl.BlockSpec((1, 1, 2) + mat + (LANE,), lambda i: (layer, i, 0, 0, 0, wb // LANE - 1))
    ospec = pl.BlockSpec((1, N_HEADS, t, HEAD_DIM), lambda i: (i, 0, 0, 0))
    oshape = jax.ShapeDtypeStruct((b, N_HEADS, t, HEAD_DIM), F32)
    return pl.pallas_call(
        functools.partial(_attn_sample_body, wb=wb, window=window, dil=dil, t=t),
        grid=(b,),
        in_specs=[qspec, nspec(1), nspec(2), cspec, tspec],
        out_specs=[ospec, ospec, tspec],
        out_shape=[oshape, oshape, jax.ShapeDtypeStruct(cache_t.shape, F32)],
        input_output_aliases={4: 2},
        compiler_params=_params(1),
        name=f"attn_sample_g{g}",
    )(q_hm, kv_t, kv_t, cache_t, stacked)


def _merge_body(x_ref, pg_ref, y_ref, r_ref, k_ref, v_ref, g_ref, o0_ref, o1_ref, o2_ref,
                l0_ref, l1_ref, l2_ref, wa_ref, wb_ref, wo_ref, gng_ref, gnb_ref, rk_ref,
                lng_ref, lnb_ref, ones_ref, h_ref, *, alpha):
    ones_bd = ones_ref[...]
    y = y_ref[...]
    v = v_ref[...]
    mu = _head_sum(y, ones_bd) * (1.0 / HEAD_DIM)
    d = y - mu
    var = _head_sum(d * d, ones_bd) * (1.0 / HEAD_DIM)
    yn = d * lax.rsqrt(var + GN_EPS) * gng_ref[...] + gnb_ref[...]
    bonus = _head_sum(r_ref[...] * k_ref[...] * rk_ref[...], ones_bd) * v
    rw = (yn + bonus) * g_ref[...]
    l0, l1, l2 = l0_ref[...], l1_ref[...], l2_ref[...]
    mx = jnp.maximum(jnp.maximum(l0, l1), l2)
    e0, e1, e2 = jnp.exp(l0 - mx), jnp.exp(l1 - mx), jnp.exp(l2 - mx)
    att = (e0 * o0_ref[...] + e1 * o1_ref[...] + e2 * o2_ref[...]) / (e0 + e1 + e2)
    pg = pg_ref[...]
    merged = (_sigmoid(pg[:, :D_MODEL]) * _dot(rw.astype(BF16), wa_ref[...])
              + _sigmoid(pg[:, D_MODEL:]) * _dot(att.astype(BF16), wb_ref[...]))
    pre = alpha * x_ref[...] + _dot(merged.astype(BF16), wo_ref[...])
    h_ref[...] = _layer_norm(pre, lng_ref[...], lnb_ref[...])


def _merge(x, p_gate, y, r, k, v, g, outs, lses, lw, tm, alpha):
    m = x.shape[0]
    row = lambda c: pl.BlockSpec((tm, c), lambda i: (i, 0))
    const = lambda a: pl.BlockSpec(a.shape, lambda i: (0,) * a.ndim)
    consts = [lw["w_br_a"], lw["w_br_b"], lw["w_out"], lw["gn_g"], lw["gn_b"], lw["rk"],
              lw["ln1_g"], lw["ln1_b"], lw["ones_bd"]]
    return pl.pallas_call(
        functools.partial(_merge_body, alpha=alpha),
        grid=(m // tm,),
        in_specs=[row(D_MODEL), row(C_GATE)] + [row(WIDTH)] * 11 + [const(a) for a in consts],
        out_specs=row(D_MODEL),
        out_shape=jax.ShapeDtypeStruct((m, D_MODEL), F32),
        compiler_params=_params(1),
        name="merge_ln1",
    )(x, p_gate, y, r, k, v, g, *outs, *lses, *consts)


def _gelu(x):
    return 0.5 * x * (1.0 + lax.erf(x * (2.0 ** -0.5)))


def _ffn_body(*refs, seq, tm, nf, carry_mode, alpha):
    if carry_mode:
        (h_ref, wu_ref, wg_ref, wd_ref, cw_ref, cb_ref, lng_ref, lnb_ref, buf_ref,
         y_ref, cnew_ref, hb_ref, acc_ref, carry_ref) = refs
    else:
        (h_ref, wu_ref, wg_ref, wd_ref, cw_ref, cb_ref, lng_ref, lnb_ref, e1_ref, e2_ref,
         y_ref, u_ref, hb_ref, acc_ref) = refs
    i = pl.program_id(0)
    j = pl.program_id(1)

    @pl.when(j == 0)
    def _():
        hb_ref[...] = h_ref[...].astype(BF16)
        acc_ref[...] = jnp.zeros_like(acc_ref)

    hb = hb_ref[...]
    u = _dot(hb, wu_ref[...])
    gate = _dot(hb, wg_ref[...])
    row = lax.broadcasted_iota(jnp.int32, u.shape, 0)
    r1 = pltpu.roll(u, 1, 0)
    r2 = pltpu.roll(u, 2, 0)
    if carry_mode:
        slot = i % 2

        @pl.when(i % (seq // tm) == 0)
        def _():
            carry_ref[1 - slot, j, 6:8, :] = buf_ref[0]

        tail = carry_ref[1 - slot, j]
        u1 = jnp.where(row == 0, tail[7:8, :], r1)
        u2 = jnp.where(row == 0, tail[6:7, :], jnp.where(row == 1, tail[7:8, :], r2))
        carry_ref[slot, j] = u[tm - 8:tm, :]
        cnew_ref[0] = u[tm - 2:tm, :]
    else:
        u1 = jnp.where(row % seq == 0, e1_ref[...], r1)
        u2 = jnp.where(row % seq < 2, e2_ref[...], r2)
        u_ref[...] = u
    cw = cw_ref[...]
    conv = cb_ref[...] + cw[0:1, :] * u2 + cw[1:2, :] * u1 + cw[2:3, :] * u
    act = _gelu(conv) * gate
    acc_ref[...] += _dot(act.astype(BF16), wd_ref[...])

    @pl.when(j == nf - 1)
    def _():
        y_ref[...] = _layer_norm(alpha * h_ref[...] + acc_ref[...], lng_ref[...], lnb_ref[...])


def _conv_ffn(h, conv_buf, lw, seq, tm, carry_mode, alpha):
    m = h.shape[0]
    b = m // seq
    nf = D_FF // FF_CHUNK
    tf = FF_CHUNK
    const = lambda a: pl.BlockSpec(a.shape, lambda i, j: (0,) * a.ndim)
    in_specs = [pl.BlockSpec((tm, D_MODEL), lambda i, j: (i, 0)),
                pl.BlockSpec((D_MODEL, tf), lambda i, j: (0, j)),
                pl.BlockSpec((D_MODEL, tf), lambda i, j: (0, nf + j)),
                pl.BlockSpec((tf, D_MODEL), lambda i, j: (j, 0)),
                pl.BlockSpec((CONV_W, tf), lambda i, j: (0, j)),
                pl.BlockSpec((1, tf), lambda i, j: (0, j)),
                const(lw["ln2_g"]), const(lw["ln2_b"])]
    args = [h, lw["w_up"], lw["w_up"], lw["w_down"], lw["conv_w"], lw["conv_b"],
            lw["ln2_g"], lw["ln2_b"]]
    scratch = [pltpu.VMEM((tm, D_MODEL), BF16), pltpu.VMEM((tm, D_MODEL), F32)]
    y_spec = pl.BlockSpec((tm, D_MODEL), lambda i, j: (i, 0))
    y_shape = jax.ShapeDtypeStruct((m, D_MODEL), F32)
    if carry_mode:
        tps = seq // tm
        in_specs.append(pl.BlockSpec((1, CONV_W - 1, tf), lambda i, j: (i // tps, 0, j)))
        args.append(conv_buf)
        out_specs = [y_spec, pl.BlockSpec((1, CONV_W - 1, tf), lambda i, j: (i, 0, j))]
        out_shape = [y_shape, jax.ShapeDtypeStruct((m // tm, CONV_W - 1, D_FF), F32)]
        scratch.append(pltpu.VMEM((2, nf, 8, tf), F32))
    else:
        zeros = jnp.zeros((b, seq, D_FF), F32)
        e1 = zeros.at[:, 0].set(conv_buf[:, 1]).reshape(m, D_FF)
        e2 = zeros.at[:, 0].set(conv_buf[:, 0]).at[:, 1].set(conv_buf[:, 1]).reshape(m, D_FF)
        tile = pl.BlockSpec((tm, tf), lambda i, j: (i, j))
        in_specs += [tile, tile]
        args += [e1, e2]
        out_specs = [y_spec, tile]
        out_shape = [y_shape, jax.ShapeDtypeStruct((m, D_FF), F32)]
    y, aux = pl.pallas_call(
        functools.partial(_ffn_body, seq=seq, tm=tm, nf=nf, carry_mode=carry_mode, alpha=alpha),
        grid=(m // tm, nf),
        in_specs=in_specs,
        out_specs=out_specs,
        out_shape=out_shape,
        scratch_shapes=scratch,
        compiler_params=_params(2),
        name="conv_ffn_ln2",
    )(*args)
    if carry_mode:
        return y, aux[seq // tm - 1::seq // tm]
    return y, aux.reshape(b, seq, D_FF)[:, seq - (CONV_W - 1):]


def _rope_tables(pos):
    half = HEAD_DIM // 2
    inv = ROPE_THETA ** (-jnp.arange(half, dtype=F32) / half)
    ang = pos.astype(F32)[:, None] * inv[None, :]
    cos, sin = jnp.cos(ang), jnp.sin(ang)
    cos = jnp.tile(jnp.concatenate([cos, cos], axis=-1), (1, N_HEADS))
    sin = jnp.tile(jnp.concatenate([-sin, sin], axis=-1), (1, N_HEADS))
    return cos, sin


def _pad_cols(a, n):
    return jnp.pad(a, ((0, 0), (0, n - a.shape[1])))


def _layer_weights(l, w):
    row = lambda a: a.reshape(1, -1)
    z = lambda r, c: jnp.zeros((r, c), F32)
    w_in = w["w_in"][l]
    if l == 0:
        vres, mu_v = z(D_MODEL, D_MV_LORA), z(1, D_MV_LORA)
        v0, v2 = z(1, WIDTH), z(D_MV_LORA, WIDTH)
    else:
        vres, mu_v = w["w_in_vres"][l - 1], row(w["mu_vres"][l - 1])
        v0, v2 = row(w["rw_v0"][l - 1]), w["rw_v2"][l - 1]
    w_rw = _pad_cols(jnp.concatenate([w_in[:, C_ATT + C_GATE:], vres], axis=1), C_RW_PAD)
    mu = _pad_cols(jnp.concatenate([row(w["mu_rw"][l]), mu_v], axis=1), C_RW_PAD)
    w2a2 = jnp.concatenate([
        jnp.concatenate([w["rw_w2"][l], z(D_DECAY_LORA, WIDTH)], axis=1),
        jnp.concatenate([z(D_AAA_LORA, WIDTH), w["rw_a2"][l]], axis=1)], axis=0)
    pad_rows = 256 - D_GATE_LORA - D_MV_LORA
    g2v2 = jnp.concatenate([
        jnp.concatenate([w["rw_g2"][l], z(D_GATE_LORA, WIDTH)], axis=1),
        jnp.concatenate([z(D_MV_LORA, WIDTH), v2], axis=1),
        z(pad_rows, 2 * WIDTH)], axis=0)
    head = jnp.arange(WIDTH) // HEAD_DIM
    return {
        "w_att": w_in[:, :C_ATT].astype(BF16),
        "w_gate": w_in[:, C_ATT:C_ATT + C_GATE].astype(BF16),
        "w_rw": w_rw.astype(BF16),
        "mu": mu, "w0": row(w["rw_w0"][l]), "a0": row(w["rw_a0"][l]), "v0": v0,
        "kk": row(w["rw_kk"][l]), "ka": row(w["rw_ka"][l]),
        "w2a2": w2a2.astype(BF16), "g2v2": g2v2.astype(BF16),
        "ones_bd": (head[:, None] == head[None, :]).astype(BF16),
        "is_l0": jnp.full((1, WIDTH), 1.0 if l == 0 else 0.0, F32),
        "gn_g": row(w["rw_gn_g"][l]), "gn_b": row(w["rw_gn_b"][l]), "rk": row(w["rw_rk"][l]),
        "w_br_a": w["w_br_a"][l].astype(BF16), "w_br_b": w["w_br_b"][l].astype(BF16),
        "w_out": w["w_out"][l].astype(BF16),
        "ln1_g": row(w["ln1_g"][l]), "ln1_b": row(w["ln1_b"][l]),
        "w_up": w["ffn_w_up"][l].astype(BF16), "w_down": w["ffn_w_down"][l].astype(BF16),
        "conv_w": w["ffn_conv_w"][l], "conv_b": row(w["ffn_conv_b"][l]),
        "ln2_g": row(w["ln2_g"][l]), "ln2_b": row(w["ln2_b"][l]),
    }


def _trunk_layer(x, b, t, x_prev, wkv0, caches, conv_buf, v_first, lw, cos, sin, alpha, layer=0,
                 stacked=None):
    m = b * t
    prompt = caches is None
    tm = min(m, 1024 if prompt else 256)
    p_att = _mm(x, lw["w_att"], tm, WIDTH)
    p_gate = _mm(x, lw["w_gate"], tm, WIDTH)
    p_rw = _mm(x, lw["w_rw"], tm, C_RW_PAD // 3)
    bp = -(-b // 8) * 8
    prev0 = _mm(jnp.pad(x_prev, ((0, bp - b), (0, 0))), lw["w_rw"], bp, C_RW_PAD // 3)[:b]

    if not prompt:
        p_nat = _rope_sample(p_att, jnp.tile(cos, (b, 1)), jnp.tile(sin, (b, 1)))
        p_nat = p_nat.reshape(b, t, 3 * N_GROUPS, N_HEADS, HEAD_DIM)
        q_hm = p_nat.transpose(0, 2, 3, 1, 4)
        kv_t = p_nat.transpose(0, 2, 3, 4, 1)
        flat = lambda a: a.transpose(0, 2, 1, 3).reshape(m, WIDTH)
    outs, lses, wins = [], [], []
    for g, (window, dil) in enumerate(ATT_GROUPS):
        if prompt:
            o, lse, kwin = _attn_prompt(p_att, cos, sin, b, t, g)
            lo = (2 * N_GROUPS + g) * WIDTH
            vwin = p_att.reshape(b, t, C_ATT)[:, t - window:, lo:lo + WIDTH]
            win = jnp.stack([kwin, vwin], axis=1).reshape(b, 2, window, N_HEADS, HEAD_DIM)
        else:
            o, lse, win = _attn_sample(q_hm, kv_t, caches[g], stacked[g], layer, b, t, g)
            o, lse = flat(o), flat(lse)
        outs.append(o)
        lses.append(lse)
        wins.append(win)

    carry_mode = t >= 512
    tm_rw = 512 if carry_mode else m
    if carry_mode:
        prev_in = prev0.reshape(b, 1, C_RW_PAD)
    else:
        prev_in = jnp.repeat(prev0, t, axis=0)
    vf_in = jnp.zeros((m, WIDTH), F32) if v_first is None else v_first
    r, lgw, k, v, kk, a, g_out = _time_mix(p_rw, prev_in, vf_in, lw, t, tm_rw, carry_mode)
    if v_first is None:
        v_first = v

    tp = -(-t // WKV_CHUNK) * WKV_CHUNK
    tok = lambda a_: jnp.pad(a_.reshape(b, t, WIDTH), ((0, 0), (0, tp - t), (0, 0)))
    y, wkv_new = _wkv(tok(r), tok(lgw), tok(k), tok(v), tok(kk), tok(a), wkv0)
    y = y[:, :t].reshape(m, WIDTH)

    tm_mg = min(m, 256)
    h = _merge(x, p_gate, y, r, k, v, g_out, outs, lses, lw, tm_mg, alpha)
    y_out, conv_new = _conv_ffn(h, conv_buf, lw, t, 1024 if carry_mode else m, carry_mode, alpha)
    shift = x.reshape(b, t, D_MODEL)[:, -1]
    return y_out, wins, wkv_new, shift, conv_new, v_first


def kernel(x_prompt, x_sample, cache_win128, cache_win512, cache_win2048, state_wkv, state_shift, state_ffn_conv, w_in, w_in_vres, mu_rw, mu_vres, rw_w0, rw_w2, rw_a0, rw_a2, rw_g2, rw_v0, rw_v2, rw_kk, rw_ka, rw_rk, rw_gn_g, rw_gn_b, w_br_a, w_br_b, w_out, ln1_g, ln1_b, ffn_w_up, ffn_conv_w, ffn_conv_b, ffn_w_down, ln2_g, ln2_b):
    w = dict(w_in=w_in, w_in_vres=w_in_vres, mu_rw=mu_rw, mu_vres=mu_vres, rw_w0=rw_w0, rw_w2=rw_w2,
             rw_a0=rw_a0, rw_a2=rw_a2, rw_g2=rw_g2, rw_v0=rw_v0, rw_v2=rw_v2, rw_kk=rw_kk,
             rw_ka=rw_ka, rw_rk=rw_rk, rw_gn_g=rw_gn_g, rw_gn_b=rw_gn_b, w_br_a=w_br_a,
             w_br_b=w_br_b, w_out=w_out, ln1_g=ln1_g, ln1_b=ln1_b, ffn_w_up=ffn_w_up,
             ffn_conv_w=ffn_conv_w, ffn_conv_b=ffn_conv_b, ffn_w_down=ffn_w_down,
             ln2_g=ln2_g, ln2_b=ln2_b)
    caches = tuple(c.transpose(0, 1, 2, 4, 5, 3) for c in (cache_win128, cache_win512, cache_win2048))
    depth = w_in.shape[0]
    alpha = ALPHA
    bp, sp, _ = x_prompt.shape
    bs, ts, _ = x_sample.shape
    cos_p, sin_p = _rope_tables(jnp.arange(sp, dtype=jnp.int32))
    cos_s, sin_s = _rope_tables(PAST_LEN + jnp.arange(ts, dtype=jnp.int32))
    xp = x_prompt.reshape(bp * sp, D_MODEL)
    xs = x_sample.reshape(bs * ts, D_MODEL)
    vf_p = vf_s = None
    win_p = [[] for _ in ATT_GROUPS]
    win_s = [_cache_roll(c, ts) for c in caches]
    wkv_p, wkv_s, sh_p, sh_s, cv_p, cv_s = [], [], [], [], [], []
    for l in range(depth):
        lw = _layer_weights(l, w)
        xp, nw, s_new, sh, cv, vf_p = _trunk_layer(
            xp, bp, sp, jnp.zeros((bp, D_MODEL), F32),
            jnp.zeros((bp, N_HEADS, HEAD_DIM, HEAD_DIM), F32), None,
            jnp.zeros((bp, CONV_W - 1, D_FF), F32), vf_p, lw, cos_p, sin_p, alpha)
        for g in range(N_GROUPS):
            win_p[g].append(nw[g])
        wkv_p.append(s_new)
        sh_p.append(sh)
        cv_p.append(cv)
        xs, win_s, s_new, sh, cv, vf_s = _trunk_layer(
            xs, bs, ts, state_shift[l], state_wkv[l], caches,
            state_ffn_conv[l], vf_s, lw, cos_s, sin_s, alpha, l, win_s)
        wkv_s.append(s_new)
        sh_s.append(sh)
        cv_s.append(cv)
    return (xp.reshape(bp, sp, D_MODEL), xs.reshape(bs, ts, D_MODEL),
            jnp.stack(win_p[0]), win_s[0].transpose(0, 1, 2, 5, 3, 4),
            jnp.stack(win_p[1]), win_s[1].transpose(0, 1, 2, 5, 3, 4),
            jnp.stack(win_p[2]), win_s[2].transpose(0, 1, 2, 5, 3, 4),
            jnp.stack(wkv_p), jnp.stack(wkv_s),
            jnp.stack(sh_p), jnp.stack(sh_s), jnp.stack(cv_p), jnp.stack(cv_s))
```

```python
import functools

import jax
import jax.numpy as jnp
from jax import lax
from jax.experimental import pallas as pl
from jax.experimental.pallas import tpu as pltpu

F32 = jnp.float32
BF16 = jnp.bfloat16

D_MODEL = 1024
HEAD_DIM = 64
N_HEADS = 8
WIDTH = N_HEADS * HEAD_DIM
ATT_GROUPS = ((128, 1), (512, 4), (2048, 16))
N_GROUPS = len(ATT_GROUPS)
BLK = 128
ROPE_THETA = 10000.0
D_DECAY_LORA = 64
D_AAA_LORA = 64
D_GATE_LORA = 160
D_MV_LORA = 32
D_FF = 2816
CONV_W = 3
LN_EPS = 1e-5
GN_EPS = 64e-5
DEPTH = 4
ALPHA = (2.0 * DEPTH) ** 0.25
PAST_LEN = 8192
C_ATT = 3 * N_GROUPS * WIDTH
C_GATE = 2 * D_MODEL
C_RW = 3 * WIDTH + D_DECAY_LORA + D_AAA_LORA + D_GATE_LORA
C_RW_PAD = 1920
LORA_WA = 3 * WIDTH
LORA_GV = LORA_WA + 128
FF_CHUNK = 256
WKV_CHUNK = 64
ATT_ROWS = 2048
ATT_GROUP = 4
LANE = 128
VMEM_LIMIT = 56 * 1024 * 1024
NEG_INF = float("-inf")


def _params(n_axes):
    return pltpu.CompilerParams(
        dimension_semantics=("arbitrary",) * n_axes, vmem_limit_bytes=VMEM_LIMIT)


def _dot(a, b):
    return jnp.dot(a, b, preferred_element_type=F32)


NN = ((1,), (0,))
NT = ((1,), (1,))
TN = ((0,), (0,))


def _dg(a, b, dims):
    return lax.dot_general(a, b, (dims, ((), ())), preferred_element_type=F32)


def _dot_nt(a, b):
    return _dg(a, b, NT)


def _split(x):
    hi = x.astype(BF16)
    return hi, (x - hi.astype(F32)).astype(BF16)


def _dot_x1(a, b, dims=NN):
    return _dg(a.astype(BF16), b.astype(BF16), dims)


def _head_sum(x, ones_bd):
    hi, lo = _split(x)
    return _dot(hi, ones_bd) + _dot(lo, ones_bd)


def _layer_norm(x, g, b):
    mu = jnp.mean(x, axis=-1, keepdims=True)
    d = x - mu
    var = jnp.mean(d * d, axis=-1, keepdims=True)
    return d * lax.rsqrt(var + LN_EPS) * g + b


def _sigmoid(x):
    return 1.0 / (1.0 + jnp.exp(-x))


def _rope(t, cos, sin):
    half = HEAD_DIM // 2
    outs = []
    for c in range(t.shape[-1] // LANE):
        tc = t[:, c * LANE:(c + 1) * LANE]
        lane = lax.broadcasted_iota(jnp.int32, tc.shape, 1)
        fwd = pltpu.roll(tc, LANE - half, 1)
        bwd = pltpu.roll(tc, half, 1)
        outs.append(jnp.where((lane & (HEAD_DIM - 1)) < half, fwd, bwd))
    partner = jnp.concatenate(outs, axis=-1)
    return t * cos + partner * sin


def _mm_body(x_ref, w_ref, o_ref, xb_ref):
    @pl.when(pl.program_id(1) == 0)
    def _():
        xb_ref[...] = x_ref[...].astype(BF16)

    o_ref[...] = _dot(xb_ref[...], w_ref[...]).astype(o_ref.dtype)


def _mm(x, w, tm, tn, out_dtype=F32):
    m, k = x.shape
    n = w.shape[1]
    return pl.pallas_call(
        _mm_body,
        grid=(m // tm, n // tn),
        in_specs=[pl.BlockSpec((tm, k), lambda i, j: (i, 0)),
                  pl.BlockSpec((k, tn), lambda i, j: (0, j))],
        out_specs=pl.BlockSpec((tm, tn), lambda i, j: (i, j)),
        out_shape=jax.ShapeDtypeStruct((m, n), out_dtype),
        scratch_shapes=[pltpu.VMEM((tm, k), BF16)],
        compiler_params=_params(2),
        name="proj_mm",
    )(x, w)


def _tm_body(*refs, seq, tm, carry_mode):
    if carry_mode:
        (p_ref, prev0_ref, mu_ref, w0_ref, a0_ref, v0_ref, kkp_ref, kap_ref, w2a2_ref, g2v2_ref,
         ones_ref, vf_ref, l0_ref,
         r_out, lw_out, k_out, v_out, kk_out, a_out, g_out, carry_ref) = refs
    else:
        (p_ref, prev0_ref, mu_ref, w0_ref, a0_ref, v0_ref, kkp_ref, kap_ref, w2a2_ref, g2v2_ref,
         ones_ref, vf_ref, l0_ref,
         r_out, lw_out, k_out, v_out, kk_out, a_out, g_out) = refs
    p = p_ref[...]
    row = lax.broadcasted_iota(jnp.int32, p.shape, 0)
    rolled = pltpu.roll(p, 1, 0)
    if carry_mode:
        slot = pl.program_id(0) % 2

        @pl.when(pl.program_id(0) % (seq // tm) == 0)
        def _():
            carry_ref[1 - slot, 0:1, :] = prev0_ref[0]

        p_prev = jnp.where(row == 0, carry_ref[1 - slot, 0:1, :], rolled)
        carry_ref[slot, 0:1, :] = p[tm - 1:tm, :]
    else:
        p_prev = jnp.where(row % seq == 0, prev0_ref[...], rolled)
    z = p + (p_prev - p) * mu_ref[...]
    r = z[:, :WIDTH]
    k = z[:, WIDTH:2 * WIDTH]
    v = z[:, 2 * WIDTH:3 * WIDTH]
    wa = z[:, LORA_WA:LORA_WA + 128]
    lane = lax.broadcasted_iota(jnp.int32, wa.shape, 1)
    wa = jnp.where(lane < D_DECAY_LORA, jnp.tanh(wa), wa)
    wa_out = _dot(wa.astype(BF16), w2a2_ref[...])
    gv = z[:, LORA_GV:LORA_GV + 256]
    lane = lax.broadcasted_iota(jnp.int32, gv.shape, 1)
    gv = jnp.where(lane < D_GATE_LORA, _sigmoid(gv), gv)
    gv_out = _dot(gv.astype(BF16), g2v2_ref[...])
    x = -(w0_ref[...] + wa_out[:, :WIDTH])
    softplus = jnp.maximum(x, 0.0) + jnp.log1p(jnp.exp(-jnp.abs(x)))
    lw_out[...] = -jnp.exp(-softplus - 0.5)
    a = _sigmoid(a0_ref[...] + wa_out[:, WIDTH:])
    g_out[...] = gv_out[:, :WIDTH]
    vf = jnp.where(l0_ref[...] > 0.5, v, vf_ref[...])
    v = v + (vf - v) * _sigmoid(v0_ref[...] + gv_out[:, WIDTH:])
    kk = k * kkp_ref[...]
    norm = jnp.sqrt(_head_sum(kk * kk, ones_ref[...]))
    kk_out[...] = kk / jnp.maximum(norm, 1e-12)
    k_out[...] = k * (1.0 + (a - 1.0) * kap_ref[...])
    r_out[...] = r
    v_out[...] = v
    a_out[...] = a


def _time_mix(p_rw, prev0, vfirst, lw, seq, tm, carry_mode):
    m = p_rw.shape[0]
    row = lambda c: pl.BlockSpec((tm, c), lambda i: (i, 0))
    const = lambda a: pl.BlockSpec(a.shape, lambda i: (0,) * a.ndim)
    if carry_mode:
        prev_spec = pl.BlockSpec((1, 1, C_RW_PAD), lambda i: (i // (seq // tm), 0, 0))
        scratch = [pltpu.VMEM((2, 8, C_RW_PAD), F32)]
    else:
        prev_spec = row(C_RW_PAD)
        scratch = []
    consts = [lw["mu"], lw["w0"], lw["a0"], lw["v0"], lw["kk"], lw["ka"], lw["w2a2"], lw["g2v2"],
              lw["ones_bd"]]
    out = jax.ShapeDtypeStruct((m, WIDTH), F32)
    return pl.pallas_call(
        functools.partial(_tm_body, seq=seq, tm=tm, carry_mode=carry_mode),
        grid=(m // tm,),
        in_specs=[row(C_RW_PAD), prev_spec] + [const(a) for a in consts]
        + [row(WIDTH), const(lw["is_l0"])],
        out_specs=[row(WIDTH)] * 7,
        out_shape=[out] * 7,
        scratch_shapes=scratch,
        compiler_params=_params(1),
        name="time_mix",
    )(p_rw, prev0, *consts, vfirst, lw["is_l0"])


def _wkv_body(r_ref, lw_ref, k_ref, v_ref, kk_ref, a_ref, s0_ref, y_ref, sout_ref, s_ref, *,
              nc, nb):
    c = pl.program_id(1)
    n = WKV_CHUNK

    @pl.when(c == 0)
    def _():
        s_ref[...] = s0_ref[...]

    row = lax.broadcasted_iota(jnp.int32, (n, n), 0)
    col = lax.broadcasted_iota(jnp.int32, (n, n), 1)
    incl = col <= row
    strict = col < row
    eye = (col == row).astype(F32)
    row2 = lax.broadcasted_iota(jnp.int32, (n, 2 * n), 0)
    col2 = lax.broadcasted_iota(jnp.int32, (n, 2 * n), 1)
    incl_cat = (col2 & (n - 1)) <= row2
    tri = incl.astype(BF16)
    where, ar, bk, vh, bk_end, g_end = [], [], [], [], [], []
    for bi in range(nb):
        lw = lw_ref[bi]
        lw_hi = lw.astype(BF16)
        lw_mid, lw_lo = _split(lw - lw_hi.astype(F32))
        cum = _dot(tri, lw_hi) + _dot(tri, lw_mid) + _dot(tri, lw_lo)
        cum_end = cum[n - 1:n, :]
        kk = kk_ref[bi]
        k = k_ref[bi]
        kka = kk * a_ref[bi]
        e_neg = jnp.exp(-cum)
        e_end = jnp.exp(cum_end - cum)
        a_t = -kk * jnp.exp(cum - lw)
        b_t = kka * e_neg
        k_t = k * e_neg
        r_t = r_ref[bi] * jnp.exp(cum)
        b_h = kka * e_end
        k_h = k * e_end
        decay_end = jnp.exp(cum_end)
        v = v_ref[bi]
        for h in range(N_HEADS):
            sl = slice(h * HEAD_DIM, (h + 1) * HEAD_DIM)
            where.append((bi, h, sl))
            ar.append(jnp.concatenate([a_t[:, sl], r_t[:, sl]], axis=0))
            bk.append(jnp.concatenate([b_t[:, sl], k_t[:, sl]], axis=0))
            bk_end.append(jnp.concatenate([b_h[:, sl], k_h[:, sl]], axis=0))
            vh.append(v[:, sl])
            g_end.append(decay_end[:, sl])
    heads = range(len(where))
    s = [s_ref[bi, h] for bi, h, _ in where]
    gram = [_dot_x1(ar[h], bk[h], NT) for h in heads]
    on_s = [_dot_x1(ar[h], s[h], NT) for h in heads]
    l_ab = [jnp.where(strict, g[:n, :n], 0.0) for g in gram]
    l_ak = [jnp.where(strict, g[:n, n:], 0.0) for g in gram]
    m_cat = [jnp.where(incl_cat, g[n:, :], 0.0) for g in gram]
    rhs = [on_s[h][:n] + _dot_x1(l_ak[h], vh[h]) for h in heads]
    acc = [eye + l for l in l_ab]
    pw = [_dot_x1(l, l) for l in l_ab]
    span = 2
    while span * 2 < n:
        both = [_dot_x1(jnp.concatenate([acc[h], pw[h]], axis=0), pw[h]) for h in heads]
        acc = [acc[h] + both[h][:n] for h in heads]
        pw = [both[h][n:] for h in heads]
        span *= 2
    acc = [acc[h] + _dot_x1(acc[h], pw[h]) for h in heads]
    u = [_dot_x1(acc[h], rhs[h]) for h in heads]
    uv = [jnp.concatenate([u[h], vh[h]], axis=0) for h in heads]
    for h, (bi, _, sl) in enumerate(where):
        y_ref[bi, :, sl] = on_s[h][n:] + _dot_x1(m_cat[h], uv[h])
    for h, (bi, hd, _) in enumerate(where):
        s_ref[bi, hd] = s[h] * g_end[h] + _dot_x1(uv[h], bk_end[h], TN)

    @pl.when(c == nc - 1)
    def _():
        sout_ref[...] = s_ref[...]


def _wkv(r, lw, k, v, kk, a, s0):
    b, t, _ = r.shape
    nc = t // WKV_CHUNK
    nb = 2 if b % 2 == 0 else 1
    tok = pl.BlockSpec((nb, WKV_CHUNK, WIDTH), lambda i, c: (i, c, 0))
    st = pl.BlockSpec((nb, N_HEADS, HEAD_DIM, HEAD_DIM), lambda i, c: (i, 0, 0, 0))
    return pl.pallas_call(
        functools.partial(_wkv_body, nc=nc, nb=nb),
        grid=(b // nb, nc),
        in_specs=[tok] * 6 + [st],
        out_specs=[tok, st],
        out_shape=[jax.ShapeDtypeStruct((b, t, WIDTH), F32),
                   jax.ShapeDtypeStruct((b, N_HEADS, HEAD_DIM, HEAD_DIM), F32)],
        scratch_shapes=[pltpu.VMEM((nb, N_HEADS, HEAD_DIM, HEAD_DIM), F32)],
        compiler_params=_params(2),
        name="wkv_chunked",
    )(r, lw, k, v, kk, a, s0)


def _attn_prompt_body(q_ref, k_ref, v_ref, cos_ref, sin_ref, o_ref, lse_ref, kwin_ref,
                      qs_ref, ks_ref, vs_ref, kc_ref, vc_ref, *, nblk, dil, window):
    j = pl.program_id(2)
    slot = j % 2
    cos = cos_ref[...]
    sin = sin_ref[...]
    qs_ref[...] = _rope(q_ref[0].astype(F32), cos, sin) * (HEAD_DIM ** -0.5)
    k_rot = _rope(k_ref[0].astype(F32), cos, sin)
    ks_ref[...] = k_rot
    vs_ref[...] = v_ref[0].astype(F32)

    @pl.when(j == nblk - 1)
    def _():
        kwin_ref[0] = k_rot[ATT_ROWS - window:, :]

    @pl.when(j == 0)
    def _():
        kc_ref[1] = jnp.zeros((dil, BLK, LANE), BF16)
        vc_ref[1] = jnp.zeros((dil, BLK, LANE), BF16)

    lane = lax.broadcasted_iota(jnp.int32, (BLK, LANE), 1)
    first = lane < HEAD_DIM
    qi = lax.broadcasted_iota(jnp.int32, (BLK, BLK), 0)
    kj = lax.broadcasted_iota(jnp.int32, (BLK, BLK), 1)
    prev_ok = kj >= qi
    prev_ok_first = (kj - jnp.where(j > 0, 0, BLK)) >= qi
    cur_ok = kj <= qi
    mcount = ATT_ROWS // (BLK * dil)
    blocks = [(r, m) for r in range(dil) for m in range(mcount)]
    kcur, vcur = {}, {}
    for g0 in range(0, len(blocks), ATT_GROUP):
        group = blocks[g0:g0 + ATT_GROUP]
        rows, q, kp, vp, kb, vb, ok = [], [], [], [], [], [], []
        for r, m in group:
            start = r + dil * BLK * m
            rw = pl.ds(start, BLK, stride=dil) if dil > 1 else pl.ds(start, BLK)
            qf = qs_ref[rw, :]
            kcur[r, m] = ks_ref[rw, :].astype(BF16)
            vcur[r, m] = vs_ref[rw, :].astype(BF16)
            k_before = kc_ref[1 - slot, r] if m == 0 else kcur[r, m - 1]
            v_before = vc_ref[1 - slot, r] if m == 0 else vcur[r, m - 1]
            for q_head in (jnp.where(first, qf, 0.0), jnp.where(first, 0.0, qf)):
                q.append(q_head.astype(BF16))
                kp.append(k_before)
                vp.append(v_before)
                kb.append(kcur[r, m])
                vb.append(vcur[r, m])
                ok.append(prev_ok_first if m == 0 else prev_ok)
            rows.append(rw)
        chains = range(len(q))
        s_prev = [jnp.where(ok[c], _dot_nt(q[c], kp[c]), NEG_INF) for c in chains]
        s_cur = [jnp.where(cur_ok, _dot_nt(q[c], kb[c]), NEG_INF) for c in chains]
        mx = [jnp.maximum(jnp.max(s_prev[c], axis=-1, keepdims=True),
                          jnp.max(s_cur[c], axis=-1, keepdims=True)) for c in chains]
        p_prev = [jnp.exp(s_prev[c] - mx[c]) for c in chains]
        p_cur = [jnp.exp(s_cur[c] - mx[c]) for c in chains]
        den = [jnp.sum(p_prev[c], axis=-1, keepdims=True)
               + jnp.sum(p_cur[c], axis=-1, keepdims=True) for c in chains]
        acc = [_dot(p_prev[c].astype(BF16), vp[c]) + _dot(p_cur[c].astype(BF16), vb[c])
               for c in chains]
        lse = [mx[c] + jnp.log(den[c]) for c in chains]
        for idx, (r, m) in enumerate(group):
            c0, c1 = 2 * idx, 2 * idx + 1
            o_ref[0, rows[idx], :] = jnp.where(first, acc[c0] / den[c0], acc[c1] / den[c1])
            lse_ref[0, rows[idx], :] = jnp.where(first, lse[c0], lse[c1])
            if m == mcount - 1:
                kc_ref[slot, r] = kcur[r, m]
                vc_ref[slot, r] = vcur[r, m]


def _attn_prompt(p_att, cos, sin, b, s, g):
    window, dil = ATT_GROUPS[g]
    assert window // dil == BLK and window <= ATT_ROWS and s % ATT_ROWS == 0
    nblk = s // ATT_ROWS
    pairs = WIDTH // LANE
    pv = p_att.reshape(b, s, C_ATT)
    blk = lambda part: pl.BlockSpec(
        (1, ATT_ROWS, LANE), lambda i, hp, j: (i, j, (part * N_GROUPS + g) * pairs + hp))
    tspec = pl.BlockSpec((ATT_ROWS, LANE), lambda i, hp, j: (j, 0))
    ospec = pl.BlockSpec((1, ATT_ROWS, LANE), lambda i, hp, j: (i, j, hp))
    o, lse, kwin = pl.pallas_call(
        functools.partial(_attn_prompt_body, nblk=nblk, dil=dil, window=window),
        grid=(b, pairs, nblk),
        in_specs=[blk(0), blk(1), blk(2), tspec, tspec],
        out_specs=[ospec, ospec, pl.BlockSpec((1, window, LANE), lambda i, hp, j: (i, 0, hp))],
        out_shape=[jax.ShapeDtypeStruct((b, s, WIDTH), F32),
                   jax.ShapeDtypeStruct((b, s, WIDTH), F32),
                   jax.ShapeDtypeStruct((b, window, WIDTH), F32)],
        scratch_shapes=[pltpu.VMEM((ATT_ROWS, LANE), F32)] * 3 + [
                        pltpu.VMEM((2, dil, BLK, LANE), BF16), pltpu.VMEM((2, dil, BLK, LANE), BF16)],
        compiler_params=_params(3),
        name=f"attn_prompt_g{g}",
    )(pv, pv, pv, cos[:, :LANE], sin[:, :LANE])
    return o.reshape(b * s, WIDTH), lse.reshape(b * s, WIDTH), kwin


def _rope_sample_body(p_ref, cos_ref, sin_ref, o_ref):
    cos = cos_ref[...]
    sin = sin_ref[...]
    for part in range(3 * N_GROUPS):
        sl = slice(part * WIDTH, (part + 1) * WIDTH)
        x = p_ref[:, sl]
        if part < N_GROUPS:
            o_ref[:, sl] = _rope(x, cos, sin) * (HEAD_DIM ** -0.5)
        elif part < 2 * N_GROUPS:
            o_ref[:, sl] = _rope(x, cos, sin)
        else:
            o_ref[:, sl] = x


def _rope_sample(p_att, cos, sin):
    m = p_att.shape[0]
    full = lambda a: pl.BlockSpec(a.shape, lambda i: (0, 0))
    return pl.pallas_call(
        _rope_sample_body,
        grid=(1,),
        in_specs=[full(p_att), full(cos), full(sin)],
        out_specs=full(p_att),
        out_shape=jax.ShapeDtypeStruct((m, C_ATT), F32),
        compiler_params=_params(1),
        name="rope_sample",
    )(p_att, cos, sin)


def _cache_roll_body(c_ref, o_ref, *, wb, t):
    for kv in range(2):
        for h in range(N_HEADS):
            x = c_ref[0, kv, h]
            o_ref[0, kv, h] = jnp.concatenate([x[:, t:], x[:, :t]], axis=1)


def _cache_roll(cache_t, t):
    depth, b = cache_t.shape[:2]
    mat = cache_t.shape[2:]
    wb = mat[-1]
    n = depth * b
    flat = cache_t.reshape((n,) + mat)
    spec = pl.BlockSpec((1,) + mat, lambda i: (i, 0, 0, 0, 0))
    out = pl.pallas_call(
        functools.partial(_cache_roll_body, wb=wb, t=t),
        grid=(n,),
        in_specs=[spec],
        out_specs=spec,
        out_shape=jax.ShapeDtypeStruct(flat.shape, F32),
        compiler_params=_params(1),
        name="cache_roll",
    )(flat)
    return out.reshape(cache_t.shape)


Q_PAD = 16


def _attn_sample_body(q_ref, kn_ref, vn_ref, c_ref, tail_ref, o_ref, lse_ref, tail_out, *,
                      wb, window, dil, t):
    heads = range(N_HEADS)
    tail_out[...] = tail_ref[...]
    tq = lax.broadcasted_iota(jnp.int32, (Q_PAD, wb), 0)
    w = lax.broadcasted_iota(jnp.int32, (Q_PAD, wb), 1)
    dist = wb + tq - w
    cache_ok = ((dist & (dil - 1)) == 0) & (dist <= window)
    tq = lax.broadcasted_iota(jnp.int32, (Q_PAD, LANE), 0)
    n = lax.broadcasted_iota(jnp.int32, (Q_PAD, LANE), 1)
    dist = tq - n
    new_ok = (n < t) & (dist >= 0) & ((dist & (dil - 1)) == 0)
    zrow = jnp.zeros((Q_PAD - t, HEAD_DIM), F32)
    zcol = jnp.zeros((HEAD_DIM, LANE - t), F32)
    q = [jnp.concatenate([q_ref[0, 0, h], zrow], axis=0).astype(BF16) for h in heads]
    kn = [jnp.concatenate([kn_ref[0, 0, h], zcol], axis=1).astype(BF16) for h in heads]
    vn = [jnp.concatenate([vn_ref[0, 0, h], zcol], axis=1).astype(BF16) for h in heads]
    s_c = [jnp.where(cache_ok, _dot(q[h], c_ref[0, 0, 0, h].astype(BF16)), NEG_INF) for h in heads]
    s_n = [jnp.where(new_ok, _dot(q[h], kn[h]), NEG_INF) for h in heads]
    mx = [jnp.maximum(jnp.max(s_c[h], axis=-1, keepdims=True),
                      jnp.max(s_n[h], axis=-1, keepdims=True)) for h in heads]
    p_c = [jnp.exp(s_c[h] - mx[h]) for h in heads]
    p_n = [jnp.exp(s_n[h] - mx[h]) for h in heads]
    den = [jnp.sum(p_c[h], axis=-1, keepdims=True) + jnp.sum(p_n[h], axis=-1, keepdims=True)
           for h in heads]
    acc = [_dot_nt(p_c[h].astype(BF16), c_ref[0, 0, 1, h].astype(BF16))
           + _dot_nt(p_n[h].astype(BF16), vn[h]) for h in heads]
    for h in heads:
        o_ref[0, h] = (acc[h] / den[h])[0:t]
        lse_ref[0, h] = jnp.broadcast_to((mx[h] + jnp.log(den[h]))[0:t], (t, HEAD_DIM))
        tail_out[0, 0, 0, h, :, LANE - t:LANE] = kn_ref[0, 0, h]
        tail_out[0, 0, 1, h, :, LANE - t:LANE] = vn_ref[0, 0, h]


def _attn_sample(q_hm, kv_t, cache_t, stacked, layer, b, t, g):
    window, dil = ATT_GROUPS[g]
    wb = cache_t.shape[-1]
    assert wb == window and wb % LANE == 0 and t <= Q_PAD
    mat = (N_HEADS, HEAD_DIM)
    qspec = pl.BlockSpec((1, 1, N_HEADS, t, HEAD_DIM), lambda i: (i, g, 0, 0, 0))
    nspec = lambda part: pl.BlockSpec((1, 1) + mat + (t,), lambda i: (i, part * N_GROUPS + g, 0, 0, 0))
    cspec = pl.BlockSpec((1, 1, 2) + mat + (wb,), lambda i: (layer, i, 0, 0, 0, 0))
    tspec = pl.BlockSpec((1, 1, 2) + mat + (LANE,), lambda i: (layer, i, 0, 0, 0, wb // LANE - 1))
    ospec = pl.BlockSpec((1, N_HEADS, t, HEAD_DIM), lambda i: (i, 0, 0, 0))
    oshape = jax.ShapeDtypeStruct((b, N_HEADS, t, HEAD_DIM), F32)
    return pl.pallas_call(
        functools.partial(_attn_sample_body, wb=wb, window=window, dil=dil, t=t),
        grid=(b,),
        in_specs=[qspec, nspec(1), nspec(2), cspec, tspec],
        out_specs=[ospec, ospec, tspec],
        out_shape=[oshape, oshape, jax.ShapeDtypeStruct(cache_t.shape, F32)],
        input_output_aliases={4: 2},
        compiler_params=_params(1),
        name=f"attn_sample_g{g}",
    )(q_hm, kv_t, kv_t, cache_t, stacked)


def _merge_body(x_ref, pg_ref, y_ref, r_ref, k_ref, v_ref, g_ref, o0_ref, o1_ref, o2_ref,
                l0_ref, l1_ref, l2_ref, wa_ref, wb_ref, wo_ref, gng_ref, gnb_ref, rk_ref,
                lng_ref, lnb_ref, ones_ref, h_ref, *, alpha):
    ones_bd = ones_ref[...]
    y = y_ref[...]
    v = v_ref[...]
    mu = _head_sum(y, ones_bd) * (1.0 / HEAD_DIM)
    d = y - mu
    var = _head_sum(d * d, ones_bd) * (1.0 / HEAD_DIM)
    yn = d * lax.rsqrt(var + GN_EPS) * gng_ref[...] + gnb_ref[...]
    bonus = _head_sum(r_ref[...] * k_ref[...] * rk_ref[...], ones_bd) * v
    rw = (yn + bonus) * g_ref[...]
    l0, l1, l2 = l0_ref[...], l1_ref[...], l2_ref[...]
    mx = jnp.maximum(jnp.maximum(l0, l1), l2)
    e0, e1, e2 = jnp.exp(l0 - mx), jnp.exp(l1 - mx), jnp.exp(l2 - mx)
    att = (e0 * o0_ref[...] + e1 * o1_ref[...] + e2 * o2_ref[...]) / (e0 + e1 + e2)
    pg = pg_ref[...].astype(F32)
    merged = (_sigmoid(pg[:, :D_MODEL]) * _dot(rw.astype(BF16), wa_ref[...])
              + _sigmoid(pg[:, D_MODEL:]) * _dot(att.astype(BF16), wb_ref[...]))
    pre = alpha * x_ref[...] + _dot(merged.astype(BF16), wo_ref[...])
    h_ref[...] = _layer_norm(pre, lng_ref[...], lnb_ref[...])


def _merge(x, p_gate, y, r, k, v, g, outs, lses, lw, tm, alpha):
    m = x.shape[0]
    row = lambda c: pl.BlockSpec((tm, c), lambda i: (i, 0))
    const = lambda a: pl.BlockSpec(a.shape, lambda i: (0,) * a.ndim)
    consts = [lw["w_br_a"], lw["w_br_b"], lw["w_out"], lw["gn_g"], lw["gn_b"], lw["rk"],
              lw["ln1_g"], lw["ln1_b"], lw["ones_bd"]]
    return pl.pallas_call(
        functools.partial(_merge_body, alpha=alpha),
        grid=(m // tm,),
        in_specs=[row(D_MODEL), row(C_GATE)] + [row(WIDTH)] * 11 + [const(a) for a in consts],
        out_specs=row(D_MODEL),
        out_shape=jax.ShapeDtypeStruct((m, D_MODEL), F32),
        compiler_params=_params(1),
        name="merge_ln1",
    )(x, p_gate, y, r, k, v, g, *outs, *lses, *consts)


def _gelu(x):
    return 0.5 * x * (1.0 + lax.erf(x * (2.0 ** -0.5)))


def _ffn_body(*refs, seq, tm, nf, carry_mode, alpha):
    if carry_mode:
        (h_ref, wu_ref, wg_ref, wd_ref, cw_ref, cb_ref, lng_ref, lnb_ref, buf_ref,
         y_ref, cnew_ref, hb_ref, act_ref, carry_ref) = refs
    else:
        (h_ref, wu_ref, wg_ref, wd_ref, cw_ref, cb_ref, lng_ref, lnb_ref, e1_ref, e2_ref,
         y_ref, u_ref, hb_ref, act_ref) = refs
    i = pl.program_id(0)
    j = pl.program_id(1)

    @pl.when(j == 0)
    def _():
        hb_ref[...] = h_ref[...].astype(BF16)

    hb = hb_ref[...]
    u = _dot(hb, wu_ref[...])
    gate = _dot(hb, wg_ref[...])
    row = lax.broadcasted_iota(jnp.int32, u.shape, 0)
    r1 = pltpu.roll(u, 1, 0)
    r2 = pltpu.roll(u, 2, 0)
    if carry_mode:
        slot = i % 2

        @pl.when(i % (seq // tm) == 0)
        def _():
            carry_ref[1 - slot, j, 6:8, :] = buf_ref[0]

        tail = carry_ref[1 - slot, j]
        u1 = jnp.where(row == 0, tail[7:8, :], r1)
        u2 = jnp.where(row == 0, tail[6:7, :], jnp.where(row == 1, tail[7:8, :], r2))
        carry_ref[slot, j] = u[tm - 8:tm, :]
        cnew_ref[0] = u[tm - 2:tm, :]
    else:
        u1 = jnp.where(row % seq == 0, e1_ref[...], r1)
        u2 = jnp.where(row % seq < 2, e2_ref[...], r2)
        u_ref[...] = u
    cw = cw_ref[...]
    conv = cb_ref[...] + cw[0:1, :] * u2 + cw[1:2, :] * u1 + cw[2:3, :] * u
    act_ref[j] = (_gelu(conv) * gate).astype(BF16)

    @pl.when(j == nf - 1)
    def _():
        down = _dot(act_ref[0], wd_ref[0])
        for c in range(1, nf):
            down = down + _dot(act_ref[c], wd_ref[c])
        y_ref[...] = _layer_norm(alpha * h_ref[...] + down, lng_ref[...], lnb_ref[...])


def _conv_ffn(h, conv_buf, lw, seq, tm, carry_mode, alpha):
    m = h.shape[0]
    b = m // seq
    nf = D_FF // FF_CHUNK
    tf = FF_CHUNK
    const = lambda a: pl.BlockSpec(a.shape, lambda i, j: (0,) * a.ndim)
    in_specs = [pl.BlockSpec((tm, D_MODEL), lambda i, j: (i, 0)),
                pl.BlockSpec((D_MODEL, tf), lambda i, j: (0, j)),
                pl.BlockSpec((D_MODEL, tf), lambda i, j: (0, nf + j)),
                pl.BlockSpec((nf, tf, D_MODEL), lambda i, j: (0, 0, 0)),
                pl.BlockSpec((CONV_W, tf), lambda i, j: (0, j)),
                pl.BlockSpec((1, tf), lambda i, j: (0, j)),
                const(lw["ln2_g"]), const(lw["ln2_b"])]
    args = [h, lw["w_up"], lw["w_up"], lw["w_down"].reshape(nf, tf, D_MODEL), lw["conv_w"],
            lw["conv_b"], lw["ln2_g"], lw["ln2_b"]]
    scratch = [pltpu.VMEM((tm, D_MODEL), BF16), pltpu.VMEM((nf, tm, tf), BF16)]
    y_spec = pl.BlockSpec((tm, D_MODEL), lambda i, j: (i, 0))
    y_shape = jax.ShapeDtypeStruct((m, D_MODEL), F32)
    if carry_mode:
        tps = seq // tm
        in_specs.append(pl.BlockSpec((1, CONV_W - 1, tf), lambda i, j: (i // tps, 0, j)))
        args.append(conv_buf)
        out_specs = [y_spec, pl.BlockSpec((1, CONV_W - 1, tf), lambda i, j: (i, 0, j))]
        out_shape = [y_shape, jax.ShapeDtypeStruct((m // tm, CONV_W - 1, D_FF), F32)]
        scratch.append(pltpu.VMEM((2, nf, 8, tf), F32))
    else:
        zeros = jnp.zeros((b, seq, D_FF), F32)
        e1 = zeros.at[:, 0].set(conv_buf[:, 1]).reshape(m, D_FF)
        e2 = zeros.at[:, 0].set(conv_buf[:, 0]).at[:, 1].set(conv_buf[:, 1]).reshape(m, D_FF)
        tile = pl.BlockSpec((tm, tf), lambda i, j: (i, j))
        in_specs += [tile, tile]
        args += [e1, e2]
        out_specs = [y_spec, tile]
        out_shape = [y_shape, jax.ShapeDtypeStruct((m, D_FF), F32)]
    y, aux = pl.pallas_call(
        functools.partial(_ffn_body, seq=seq, tm=tm, nf=nf, carry_mode=carry_mode, alpha=alpha),
        grid=(m // tm, nf),
        in_specs=in_specs,
        out_specs=out_specs,
        out_shape=out_shape,
        scratch_shapes=scratch,
        compiler_params=_params(2),
        name="conv_ffn_ln2",
    )(*args)
    if carry_mode:
        return y, aux[seq // tm - 1::seq // tm]
    return y, aux.reshape(b, seq, D_FF)[:, seq - (CONV_W - 1):]


def _rope_tables(pos):
    half = HEAD_DIM // 2
    inv = ROPE_THETA ** (-jnp.arange(half, dtype=F32) / half)
    ang = pos.astype(F32)[:, None] * inv[None, :]
    cos, sin = jnp.cos(ang), jnp.sin(ang)
    cos = jnp.tile(jnp.concatenate([cos, cos], axis=-1), (1, N_HEADS))
    sin = jnp.tile(jnp.concatenate([-sin, sin], axis=-1), (1, N_HEADS))
    return cos, sin


def _pad_cols(a, n):
    return jnp.pad(a, ((0, 0), (0, n - a.shape[1])))


def _layer_weights(l, w):
    row = lambda a: a.reshape(1, -1)
    z = lambda r, c: jnp.zeros((r, c), F32)
    w_in = w["w_in"][l]
    if l == 0:
        vres, mu_v = z(D_MODEL, D_MV_LORA), z(1, D_MV_LORA)
        v0, v2 = z(1, WIDTH), z(D_MV_LORA, WIDTH)
    else:
        vres, mu_v = w["w_in_vres"][l - 1], row(w["mu_vres"][l - 1])
        v0, v2 = row(w["rw_v0"][l - 1]), w["rw_v2"][l - 1]
    w_rw = _pad_cols(jnp.concatenate([w_in[:, C_ATT + C_GATE:], vres], axis=1), C_RW_PAD)
    mu = _pad_cols(jnp.concatenate([row(w["mu_rw"][l]), mu_v], axis=1), C_RW_PAD)
    w2a2 = jnp.concatenate([
        jnp.concatenate([w["rw_w2"][l], z(D_DECAY_LORA, WIDTH)], axis=1),
        jnp.concatenate([z(D_AAA_LORA, WIDTH), w["rw_a2"][l]], axis=1)], axis=0)
    pad_rows = 256 - D_GATE_LORA - D_MV_LORA
    g2v2 = jnp.concatenate([
        jnp.concatenate([w["rw_g2"][l], z(D_GATE_LORA, WIDTH)], axis=1),
        jnp.concatenate([z(D_MV_LORA, WIDTH), v2], axis=1),
        z(pad_rows, 2 * WIDTH)], axis=0)
    head = jnp.arange(WIDTH) // HEAD_DIM
    return {
        "w_att": w_in[:, :C_ATT].astype(BF16),
        "w_gate": w_in[:, C_ATT:C_ATT + C_GATE].astype(BF16),
        "w_rw": w_rw.astype(BF16),
        "mu": mu, "w0": row(w["rw_w0"][l]), "a0": row(w["rw_a0"][l]), "v0": v0,
        "kk": row(w["rw_kk"][l]), "ka": row(w["rw_ka"][l]),
        "w2a2": w2a2.astype(BF16), "g2v2": g2v2.astype(BF16),
        "ones_bd": (head[:, None] == head[None, :]).astype(BF16),
        "is_l0": jnp.full((1, WIDTH), 1.0 if l == 0 else 0.0, F32),
        "gn_g": row(w["rw_gn_g"][l]), "gn_b": row(w["rw_gn_b"][l]), "rk": row(w["rw_rk"][l]),
        "w_br_a": w["w_br_a"][l].astype(BF16), "w_br_b": w["w_br_b"][l].astype(BF16),
        "w_out": w["w_out"][l].astype(BF16),
        "ln1_g": row(w["ln1_g"][l]), "ln1_b": row(w["ln1_b"][l]),
        "w_up": w["ffn_w_up"][l].astype(BF16), "w_down": w["ffn_w_down"][l].astype(BF16),
        "conv_w": w["ffn_conv_w"][l], "conv_b": row(w["ffn_conv_b"][l]),
        "ln2_g": row(w["ln2_g"][l]), "ln2_b": row(w["ln2_b"][l]),
    }


def _trunk_layer(x, b, t, x_prev, wkv0, caches, conv_buf, v_first, lw, cos, sin, alpha, layer=0,
                 stacked=None):
    m = b * t
    prompt = caches is None
    tm = min(m, 1024 if prompt else 256)
    act_dtype = BF16 if prompt else F32
    p_att = _mm(x, lw["w_att"], tm, WIDTH, act_dtype)
    p_gate = _mm(x, lw["w_gate"], tm, WIDTH, act_dtype)
    p_rw = _mm(x, lw["w_rw"], tm, C_RW_PAD // 3)
    bp = -(-b // 8) * 8
    prev0 = _mm(jnp.pad(x_prev, ((0, bp - b), (0, 0))), lw["w_rw"], bp, C_RW_PAD // 3)[:b]

    if not prompt:
        p_nat = _rope_sample(p_att, jnp.tile(cos, (b, 1)), jnp.tile(sin, (b, 1)))
        p_nat = p_nat.reshape(b, t, 3 * N_GROUPS, N_HEADS, HEAD_DIM)
        q_hm = p_nat.transpose(0, 2, 3, 1, 4)
        kv_t = p_nat.transpose(0, 2, 3, 4, 1)
        flat = lambda a: a.transpose(0, 2, 1, 3).reshape(m, WIDTH)
    outs, lses, wins = [], [], []
    for g, (window, dil) in enumerate(ATT_GROUPS):
        if prompt:
            o, lse, kwin = _attn_prompt(p_att, cos, sin, b, t, g)
            lo = (2 * N_GROUPS + g) * WIDTH
            vwin = p_att.reshape(b, t, C_ATT)[:, t - window:, lo:lo + WIDTH].astype(F32)
            win = jnp.stack([kwin, vwin], axis=1).reshape(b, 2, window, N_HEADS, HEAD_DIM)
        else:
            o, lse, win = _attn_sample(q_hm, kv_t, caches[g], stacked[g], layer, b, t, g)
            o, lse = flat(o), flat(lse)
        outs.append(o)
        lses.append(lse)
        wins.append(win)

    carry_mode = t >= 512
    tm_rw = 512 if carry_mode else m
    if carry_mode:
        prev_in = prev0.reshape(b, 1, C_RW_PAD)
    else:
        prev_in = jnp.repeat(prev0, t, axis=0)
    vf_in = jnp.zeros((m, WIDTH), F32) if v_first is None else v_first
    r, lgw, k, v, kk, a, g_out = _time_mix(p_rw, prev_in, vf_in, lw, t, tm_rw, carry_mode)
    if v_first is None:
        v_first = v

    tp = -(-t // WKV_CHUNK) * WKV_CHUNK
    tok = lambda a_: jnp.pad(a_.reshape(b, t, WIDTH), ((0, 0), (0, tp - t), (0, 0)))
    y, wkv_new = _wkv(tok(r), tok(lgw), tok(k), tok(v), tok(kk), tok(a), wkv0)
    y = y[:, :t].reshape(m, WIDTH)

    tm_mg = min(m, 256)
    h = _merge(x, p_gate, y, r, k, v, g_out, outs, lses, lw, tm_mg, alpha)
    y_out, conv_new = _conv_ffn(h, conv_buf, lw, t, 1024 if carry_mode else m, carry_mode, alpha)
    shift = x.reshape(b, t, D_MODEL)[:, -1]
    return y_out, wins, wkv_new, shift, conv_new, v_first


def kernel(x_prompt, x_sample, cache_win128, cache_win512, cache_win2048, state_wkv, state_shift, state_ffn_conv, w_in, w_in_vres, mu_rw, mu_vres, rw_w0, rw_w2, rw_a0, rw_a2, rw_g2, rw_v0, rw_v2, rw_kk, rw_ka, rw_rk, rw_gn_g, rw_gn_b, w_br_a, w_br_b, w_out, ln1_g, ln1_b, ffn_w_up, ffn_conv_w, ffn_conv_b, ffn_w_down, ln2_g, ln2_b):
    w = dict(w_in=w_in, w_in_vres=w_in_vres, mu_rw=mu_rw, mu_vres=mu_vres, rw_w0=rw_w0, rw_w2=rw_w2,
             rw_a0=rw_a0, rw_a2=rw_a2, rw_g2=rw_g2, rw_v0=rw_v0, rw_v2=rw_v2, rw_kk=rw_kk,
             rw_ka=rw_ka, rw_rk=rw_rk, rw_gn_g=rw_gn_g, rw_gn_b=rw_gn_b, w_br_a=w_br_a,
             w_br_b=w_br_b, w_out=w_out, ln1_g=ln1_g, ln1_b=ln1_b, ffn_w_up=ffn_w_up,
             ffn_conv_w=ffn_conv_w, ffn_conv_b=ffn_conv_b, ffn_w_down=ffn_w_down,
             ln2_g=ln2_g, ln2_b=ln2_b)
    caches = tuple(c.transpose(0, 1, 2, 4, 5, 3) for c in (cache_win128, cache_win512, cache_win2048))
    depth = w_in.shape[0]
    alpha = ALPHA
    bp, sp, _ = x_prompt.shape
    bs, ts, _ = x_sample.shape
    cos_p, sin_p = _rope_tables(jnp.arange(sp, dtype=jnp.int32))
    cos_s, sin_s = _rope_tables(PAST_LEN + jnp.arange(ts, dtype=jnp.int32))
    xp = x_prompt.reshape(bp * sp, D_MODEL)
    xs = x_sample.reshape(bs * ts, D_MODEL)
    vf_p = vf_s = None
    win_p = [[] for _ in ATT_GROUPS]
    win_s = [_cache_roll(c, ts) for c in caches]
    wkv_p, wkv_s, sh_p, sh_s, cv_p, cv_s = [], [], [], [], [], []
    for l in range(depth):
        lw = _layer_weights(l, w)
        xp, nw, s_new, sh, cv, vf_p = _trunk_layer(
            xp, bp, sp, jnp.zeros((bp, D_MODEL), F32),
            jnp.zeros((bp, N_HEADS, HEAD_DIM, HEAD_DIM), F32), None,
            jnp.zeros((bp, CONV_W - 1, D_FF), F32), vf_p, lw, cos_p, sin_p, alpha)
        for g in range(N_GROUPS):
            win_p[g].append(nw[g])
        wkv_p.append(s_new)
        sh_p.append(sh)
        cv_p.append(cv)
        xs, win_s, s_new, sh, cv, vf_s = _trunk_layer(
            xs, bs, ts, state_shift[l], state_wkv[l], caches,
            state_ffn_conv[l], vf_s, lw, cos_s, sin_s, alpha, l, win_s)
        wkv_s.append(s_new)
        sh_s.append(sh)
        cv_s.append(cv)
    return (xp.reshape(bp, sp, D_MODEL), xs.reshape(bs, ts, D_MODEL),
            jnp.stack(win_p[0]), win_s[0].transpose(0, 1, 2, 5, 3, 4),
            jnp.stack(win_p[1]), win_s[1].transpose(0, 1, 2, 5, 3, 4),
            jnp.stack(win_p[2]), win_s[2].transpose(0, 1, 2, 5, 3, 4),
            jnp.stack(wkv_p), jnp.stack(wkv_s),
            jnp.stack(sh_p), jnp.stack(sh_s), jnp.stack(cv_p), jnp.stack(cv_s))
```

```python
import functools

import jax
import jax.numpy as jnp
from jax import lax
from jax.experimental import pallas as pl
from jax.experimental.pallas import tpu as pltpu

F32 = jnp.float32
BF16 = jnp.bfloat16

D_MODEL = 1024
HEAD_DIM = 64
N_HEADS = 8
WIDTH = N_HEADS * HEAD_DIM
ATT_GROUPS = ((128, 1), (512, 4), (2048, 16))
N_GROUPS = len(ATT_GROUPS)
BLK = 128
ROPE_THETA = 10000.0
D_DECAY_LORA = 64
D_AAA_LORA = 64
D_GATE_LORA = 160
D_MV_LORA = 32
D_FF = 2816
CONV_W = 3
LN_EPS = 1e-5
GN_EPS = 64e-5
DEPTH = 4
ALPHA = (2.0 * DEPTH) ** 0.25
PAST_LEN = 8192
C_ATT = 3 * N_GROUPS * WIDTH
C_GATE = 2 * D_MODEL
C_RW = 3 * WIDTH + D_DECAY_LORA + D_AAA_LORA + D_GATE_LORA
C_RW_PAD = 1920
LORA_WA = 3 * WIDTH
LORA_GV = LORA_WA + 128
FF_CHUNK = 256
WKV_CHUNK = 64
ATT_ROWS = 2048
ATT_GROUP = 8
LANE = 128
VMEM_LIMIT = 56 * 1024 * 1024
NEG_INF = float("-inf")


def _params(n_axes):
    return pltpu.CompilerParams(
        dimension_semantics=("arbitrary",) * n_axes, vmem_limit_bytes=VMEM_LIMIT)


def _dot(a, b):
    return jnp.dot(a, b, preferred_element_type=F32)


NN = ((1,), (0,))
NT = ((1,), (1,))
TN = ((0,), (0,))


def _dg(a, b, dims):
    return lax.dot_general(a, b, (dims, ((), ())), preferred_element_type=F32)


def _dot_nt(a, b):
    return _dg(a, b, NT)


def _split(x):
    hi = x.astype(BF16)
    return hi, (x - hi.astype(F32)).astype(BF16)


def _dot_x1(a, b, dims=NN):
    return _dg(a.astype(BF16), b.astype(BF16), dims)


def _head_sum(x, ones_bd):
    hi, lo = _split(x)
    return _dot(hi, ones_bd) + _dot(lo, ones_bd)


def _layer_norm(x, g, b):
    mu = jnp.mean(x, axis=-1, keepdims=True)
    d = x - mu
    var = jnp.mean(d * d, axis=-1, keepdims=True)
    return d * lax.rsqrt(var + LN_EPS) * g + b


def _sigmoid(x):
    return 1.0 / (1.0 + jnp.exp(-x))


def _rope(t, cos, sin):
    half = HEAD_DIM // 2
    outs = []
    for c in range(t.shape[-1] // LANE):
        tc = t[:, c * LANE:(c + 1) * LANE]
        lane = lax.broadcasted_iota(jnp.int32, tc.shape, 1)
        fwd = pltpu.roll(tc, LANE - half, 1)
        bwd = pltpu.roll(tc, half, 1)
        outs.append(jnp.where((lane & (HEAD_DIM - 1)) < half, fwd, bwd))
    partner = jnp.concatenate(outs, axis=-1)
    return t * cos + partner * sin


def _mm_body(x_ref, w_ref, o_ref, xb_ref):
    @pl.when(pl.program_id(1) == 0)
    def _():
        xb_ref[...] = x_ref[...].astype(BF16)

    o_ref[...] = _dot(xb_ref[...], w_ref[...]).astype(o_ref.dtype)


def _mm(x, w, tm, tn, out_dtype=F32):
    m, k = x.shape
    n = w.shape[1]
    return pl.pallas_call(
        _mm_body,
        grid=(m // tm, n // tn),
        in_specs=[pl.BlockSpec((tm, k), lambda i, j: (i, 0)),
                  pl.BlockSpec((k, tn), lambda i, j: (0, j))],
        out_specs=pl.BlockSpec((tm, tn), lambda i, j: (i, j)),
        out_shape=jax.ShapeDtypeStruct((m, n), out_dtype),
        scratch_shapes=[pltpu.VMEM((tm, k), BF16)],
        compiler_params=_params(2),
        name="proj_mm",
    )(x, w)


def _tm_body(*refs, seq, tm, carry_mode):
    if carry_mode:
        (p_ref, prev0_ref, mu_ref, w0_ref, a0_ref, v0_ref, kkp_ref, kap_ref, w2a2_ref, g2v2_ref,
         ones_ref, vf_ref, l0_ref,
         r_out, lw_out, k_out, v_out, kk_out, a_out, g_out, carry_ref) = refs
    else:
        (p_ref, prev0_ref, mu_ref, w0_ref, a0_ref, v0_ref, kkp_ref, kap_ref, w2a2_ref, g2v2_ref,
         ones_ref, vf_ref, l0_ref,
         r_out, lw_out, k_out, v_out, kk_out, a_out, g_out) = refs
    p = p_ref[...]
    row = lax.broadcasted_iota(jnp.int32, p.shape, 0)
    rolled = pltpu.roll(p, 1, 0)
    if carry_mode:
        slot = pl.program_id(0) % 2

        @pl.when(pl.program_id(0) % (seq // tm) == 0)
        def _():
            carry_ref[1 - slot, 0:1, :] = prev0_ref[0]

        p_prev = jnp.where(row == 0, carry_ref[1 - slot, 0:1, :], rolled)
        carry_ref[slot, 0:1, :] = p[tm - 1:tm, :]
    else:
        p_prev = jnp.where(row % seq == 0, prev0_ref[...], rolled)
    z = p + (p_prev - p) * mu_ref[...]
    r = z[:, :WIDTH]
    k = z[:, WIDTH:2 * WIDTH]
    v = z[:, 2 * WIDTH:3 * WIDTH]
    wa = z[:, LORA_WA:LORA_WA + 128]
    lane = lax.broadcasted_iota(jnp.int32, wa.shape, 1)
    wa = jnp.where(lane < D_DECAY_LORA, jnp.tanh(wa), wa)
    wa_out = _dot(wa.astype(BF16), w2a2_ref[...])
    gv = z[:, LORA_GV:LORA_GV + 256]
    lane = lax.broadcasted_iota(jnp.int32, gv.shape, 1)
    gv = jnp.where(lane < D_GATE_LORA, _sigmoid(gv), gv)
    gv_out = _dot(gv.astype(BF16), g2v2_ref[...])
    x = -(w0_ref[...] + wa_out[:, :WIDTH])
    softplus = jnp.maximum(x, 0.0) + jnp.log1p(jnp.exp(-jnp.abs(x)))
    lw_out[...] = -jnp.exp(-softplus - 0.5)
    a = _sigmoid(a0_ref[...] + wa_out[:, WIDTH:])
    g_out[...] = gv_out[:, :WIDTH]
    vf = jnp.where(l0_ref[...] > 0.5, v, vf_ref[...])
    v = v + (vf - v) * _sigmoid(v0_ref[...] + gv_out[:, WIDTH:])
    kk = k * kkp_ref[...]
    norm = jnp.sqrt(_head_sum(kk * kk, ones_ref[...]))
    kk_out[...] = kk / jnp.maximum(norm, 1e-12)
    k_out[...] = k * (1.0 + (a - 1.0) * kap_ref[...])
    r_out[...] = r
    v_out[...] = v
    a_out[...] = a


def _time_mix(p_rw, prev0, vfirst, lw, seq, tm, carry_mode):
    m = p_rw.shape[0]
    row = lambda c: pl.BlockSpec((tm, c), lambda i: (i, 0))
    const = lambda a: pl.BlockSpec(a.shape, lambda i: (0,) * a.ndim)
    if carry_mode:
        prev_spec = pl.BlockSpec((1, 1, C_RW_PAD), lambda i: (i // (seq // tm), 0, 0))
        scratch = [pltpu.VMEM((2, 8, C_RW_PAD), F32)]
    else:
        prev_spec = row(C_RW_PAD)
        scratch = []
    consts = [lw["mu"], lw["w0"], lw["a0"], lw["v0"], lw["kk"], lw["ka"], lw["w2a2"], lw["g2v2"],
              lw["ones_bd"]]
    out = jax.ShapeDtypeStruct((m, WIDTH), F32)
    return pl.pallas_call(
        functools.partial(_tm_body, seq=seq, tm=tm, carry_mode=carry_mode),
        grid=(m // tm,),
        in_specs=[row(C_RW_PAD), prev_spec] + [const(a) for a in consts]
        + [row(WIDTH), const(lw["is_l0"])],
        out_specs=[row(WIDTH)] * 7,
        out_shape=[out] * 7,
        scratch_shapes=scratch,
        compiler_params=_params(1),
        name="time_mix",
    )(p_rw, prev0, *consts, vfirst, lw["is_l0"])


def _wkv_body(r_ref, lw_ref, k_ref, v_ref, kk_ref, a_ref, s0_ref, y_ref, sout_ref, s_ref, *,
              nc, nb):
    c = pl.program_id(1)
    n = WKV_CHUNK

    @pl.when(c == 0)
    def _():
        s_ref[...] = s0_ref[...]

    row = lax.broadcasted_iota(jnp.int32, (n, n), 0)
    col = lax.broadcasted_iota(jnp.int32, (n, n), 1)
    incl = col <= row
    strict = col < row
    eye = (col == row).astype(F32)
    row2 = lax.broadcasted_iota(jnp.int32, (n, 2 * n), 0)
    col2 = lax.broadcasted_iota(jnp.int32, (n, 2 * n), 1)
    incl_cat = (col2 & (n - 1)) <= row2
    tri = incl.astype(BF16)
    where, ar, bk, vh, bk_end, g_end = [], [], [], [], [], []
    for bi in range(nb):
        lw = lw_ref[bi]
        lw_hi = lw.astype(BF16)
        lw_mid, lw_lo = _split(lw - lw_hi.astype(F32))
        cum = _dot(tri, lw_hi) + _dot(tri, lw_mid) + _dot(tri, lw_lo)
        cum_end = cum[n - 1:n, :]
        kk = kk_ref[bi]
        k = k_ref[bi]
        kka = kk * a_ref[bi]
        e_neg = jnp.exp(-cum)
        e_end = jnp.exp(cum_end - cum)
        a_t = -kk * jnp.exp(cum - lw)
        b_t = kka * e_neg
        k_t = k * e_neg
        r_t = r_ref[bi] * jnp.exp(cum)
        b_h = kka * e_end
        k_h = k * e_end
        decay_end = jnp.exp(cum_end)
        v = v_ref[bi]
        for h in range(N_HEADS):
            sl = slice(h * HEAD_DIM, (h + 1) * HEAD_DIM)
            where.append((bi, h, sl))
            ar.append(jnp.concatenate([a_t[:, sl], r_t[:, sl]], axis=0))
            bk.append(jnp.concatenate([b_t[:, sl], k_t[:, sl]], axis=0))
            bk_end.append(jnp.concatenate([b_h[:, sl], k_h[:, sl]], axis=0))
            vh.append(v[:, sl])
            g_end.append(decay_end[:, sl])
    heads = range(len(where))
    s = [s_ref[bi, h] for bi, h, _ in where]
    gram = [_dot_x1(ar[h], bk[h], NT) for h in heads]
    on_s = [_dot_x1(ar[h], s[h], NT) for h in heads]
    l_ab = [jnp.where(strict, g[:n, :n], 0.0) for g in gram]
    l_ak = [jnp.where(strict, g[:n, n:], 0.0) for g in gram]
    m_cat = [jnp.where(incl_cat, g[n:, :], 0.0) for g in gram]
    rhs = [on_s[h][:n] + _dot_x1(l_ak[h], vh[h]) for h in heads]
    acc = [eye + l for l in l_ab]
    pw = [_dot_x1(l, l) for l in l_ab]
    span = 2
    while span * 2 < n:
        both = [_dot_x1(jnp.concatenate([acc[h], pw[h]], axis=0), pw[h]) for h in heads]
        acc = [acc[h] + both[h][:n] for h in heads]
        pw = [both[h][n:] for h in heads]
        span *= 2
    acc = [acc[h] + _dot_x1(acc[h], pw[h]) for h in heads]
    u = [_dot_x1(acc[h], rhs[h]) for h in heads]
    uv = [jnp.concatenate([u[h], vh[h]], axis=0) for h in heads]
    for h, (bi, _, sl) in enumerate(where):
        y_ref[bi, :, sl] = on_s[h][n:] + _dot_x1(m_cat[h], uv[h])
    for h, (bi, hd, _) in enumerate(where):
        s_ref[bi, hd] = s[h] * g_end[h] + _dot_x1(uv[h], bk_end[h], TN)

    @pl.when(c == nc - 1)
    def _():
        sout_ref[...] = s_ref[...]


def _wkv(r, lw, k, v, kk, a, s0):
    b, t, _ = r.shape
    nc = t // WKV_CHUNK
    nb = 2 if b % 2 == 0 else 1
    tok = pl.BlockSpec((nb, WKV_CHUNK, WIDTH), lambda i, c: (i, c, 0))
    st = pl.BlockSpec((nb, N_HEADS, HEAD_DIM, HEAD_DIM), lambda i, c: (i, 0, 0, 0))
    return pl.pallas_call(
        functools.partial(_wkv_body, nc=nc, nb=nb),
        grid=(b // nb, nc),
        in_specs=[tok] * 6 + [st],
        out_specs=[tok, st],
        out_shape=[jax.ShapeDtypeStruct((b, t, WIDTH), F32),
                   jax.ShapeDtypeStruct((b, N_HEADS, HEAD_DIM, HEAD_DIM), F32)],
        scratch_shapes=[pltpu.VMEM((nb, N_HEADS, HEAD_DIM, HEAD_DIM), F32)],
        compiler_params=_params(2),
        name="wkv_chunked",
    )(r, lw, k, v, kk, a, s0)


def _attn_prompt_body(q_ref, k_ref, v_ref, cos_ref, sin_ref, o_ref, lse_ref, kwin_ref,
                      qs_ref, ks_ref, vs_ref, kc_ref, vc_ref, *, nblk, dil, window):
    j = pl.program_id(2)
    slot = j % 2
    cos = cos_ref[...]
    sin = sin_ref[...]
    qs_ref[...] = _rope(q_ref[0].astype(F32), cos, sin) * (HEAD_DIM ** -0.5)
    k_rot = _rope(k_ref[0].astype(F32), cos, sin)
    ks_ref[...] = k_rot
    vs_ref[...] = v_ref[0].astype(F32)

    @pl.when(j == nblk - 1)
    def _():
        kwin_ref[0] = k_rot[ATT_ROWS - window:, :]

    @pl.when(j == 0)
    def _():
        kc_ref[1] = jnp.zeros((dil, BLK, LANE), BF16)
        vc_ref[1] = jnp.zeros((dil, BLK, LANE), BF16)

    lane = lax.broadcasted_iota(jnp.int32, (BLK, LANE), 1)
    first = lane < HEAD_DIM
    qi = lax.broadcasted_iota(jnp.int32, (BLK, BLK), 0)
    kj = lax.broadcasted_iota(jnp.int32, (BLK, BLK), 1)
    prev_ok = kj >= qi
    prev_ok_first = (kj - jnp.where(j > 0, 0, BLK)) >= qi
    cur_ok = kj <= qi
    mcount = ATT_ROWS // (BLK * dil)
    blocks = [(r, m) for r in range(dil) for m in range(mcount)]
    kcur, vcur = {}, {}
    for g0 in range(0, len(blocks), ATT_GROUP):
        group = blocks[g0:g0 + ATT_GROUP]
        rows, q, kp, vp, kb, vb, ok = [], [], [], [], [], [], []
        for r, m in group:
            start = r + dil * BLK * m
            rw = pl.ds(start, BLK, stride=dil) if dil > 1 else pl.ds(start, BLK)
            qf = qs_ref[rw, :]
            kcur[r, m] = ks_ref[rw, :].astype(BF16)
            vcur[r, m] = vs_ref[rw, :].astype(BF16)
            k_before = kc_ref[1 - slot, r] if m == 0 else kcur[r, m - 1]
            v_before = vc_ref[1 - slot, r] if m == 0 else vcur[r, m - 1]
            for q_head in (jnp.where(first, qf, 0.0), jnp.where(first, 0.0, qf)):
                q.append(q_head.astype(BF16))
                kp.append(k_before)
                vp.append(v_before)
                kb.append(kcur[r, m])
                vb.append(vcur[r, m])
                ok.append(prev_ok_first if m == 0 else prev_ok)
            rows.append(rw)
        chains = range(len(q))
        s_prev = [jnp.where(ok[c], _dot_nt(q[c], kp[c]), NEG_INF) for c in chains]
        s_cur = [jnp.where(cur_ok, _dot_nt(q[c], kb[c]), NEG_INF) for c in chains]
        mx = [jnp.maximum(jnp.max(s_prev[c], axis=-1, keepdims=True),
                          jnp.max(s_cur[c], axis=-1, keepdims=True)) for c in chains]
        p_prev = [jnp.exp(s_prev[c] - mx[c]) for c in chains]
        p_cur = [jnp.exp(s_cur[c] - mx[c]) for c in chains]
        den = [jnp.sum(p_prev[c], axis=-1, keepdims=True)
               + jnp.sum(p_cur[c], axis=-1, keepdims=True) for c in chains]
        acc = [_dot(p_prev[c].astype(BF16), vp[c]) + _dot(p_cur[c].astype(BF16), vb[c])
               for c in chains]
        lse = [mx[c] + jnp.log(den[c]) for c in chains]
        for idx, (r, m) in enumerate(group):
            c0, c1 = 2 * idx, 2 * idx + 1
            o_ref[0, rows[idx], :] = jnp.where(first, acc[c0] / den[c0], acc[c1] / den[c1])
            lse_ref[0, rows[idx], :] = jnp.where(first, lse[c0], lse[c1])
            if m == mcount - 1:
                kc_ref[slot, r] = kcur[r, m]
                vc_ref[slot, r] = vcur[r, m]


def _attn_prompt(p_att, cos, sin, b, s, g):
    window, dil = ATT_GROUPS[g]
    assert window // dil == BLK and window <= ATT_ROWS and s % ATT_ROWS == 0
    nblk = s // ATT_ROWS
    pairs = WIDTH // LANE
    pv = p_att.reshape(b, s, C_ATT)
    blk = lambda part: pl.BlockSpec(
        (1, ATT_ROWS, LANE), lambda i, hp, j: (i, j, (part * N_GROUPS + g) * pairs + hp))
    tspec = pl.BlockSpec((ATT_ROWS, LANE), lambda i, hp, j: (j, 0))
    ospec = pl.BlockSpec((1, ATT_ROWS, LANE), lambda i, hp, j: (i, j, hp))
    o, lse, kwin = pl.pallas_call(
        functools.partial(_attn_prompt_body, nblk=nblk, dil=dil, window=window),
        grid=(b, pairs, nblk),
        in_specs=[blk(0), blk(1), blk(2), tspec, tspec],
        out_specs=[ospec, ospec, pl.BlockSpec((1, window, LANE), lambda i, hp, j: (i, 0, hp))],
        out_shape=[jax.ShapeDtypeStruct((b, s, WIDTH), F32),
                   jax.ShapeDtypeStruct((b, s, WIDTH), F32),
                   jax.ShapeDtypeStruct((b, window, WIDTH), F32)],
        scratch_shapes=[pltpu.VMEM((ATT_ROWS, LANE), F32)] * 3 + [
                        pltpu.VMEM((2, dil, BLK, LANE), BF16), pltpu.VMEM((2, dil, BLK, LANE), BF16)],
        compiler_params=_params(3),
        name=f"attn_prompt_g{g}",
    )(pv, pv, pv, cos[:, :LANE], sin[:, :LANE])
    return o.reshape(b * s, WIDTH), lse.reshape(b * s, WIDTH), kwin


def _rope_sample_body(p_ref, cos_ref, sin_ref, o_ref):
    cos = cos_ref[...]
    sin = sin_ref[...]
    for part in range(3 * N_GROUPS):
        sl = slice(part * WIDTH, (part + 1) * WIDTH)
        x = p_ref[:, sl]
        if part < N_GROUPS:
            o_ref[:, sl] = _rope(x, cos, sin) * (HEAD_DIM ** -0.5)
        elif part < 2 * N_GROUPS:
            o_ref[:, sl] = _rope(x, cos, sin)
        else:
            o_ref[:, sl] = x


def _rope_sample(p_att, cos, sin):
    m = p_att.shape[0]
    full = lambda a: pl.BlockSpec(a.shape, lambda i: (0, 0))
    return pl.pallas_call(
        _rope_sample_body,
        grid=(1,),
        in_specs=[full(p_att), full(cos), full(sin)],
        out_specs=full(p_att),
        out_shape=jax.ShapeDtypeStruct((m, C_ATT), F32),
        compiler_params=_params(1),
        name="rope_sample",
    )(p_att, cos, sin)


def _cache_roll_body(c_ref, o_ref, *, wb, t):
    for kv in range(2):
        for h in range(N_HEADS):
            x = c_ref[0, kv, h]
            o_ref[0, kv, h] = jnp.concatenate([x[:, t:], x[:, :t]], axis=1)


def _cache_roll(cache_t, t):
    depth, b = cache_t.shape[:2]
    mat = cache_t.shape[2:]
    wb = mat[-1]
    n = depth * b
    flat = cache_t.reshape((n,) + mat)
    spec = pl.BlockSpec((1,) + mat, lambda i: (i, 0, 0, 0, 0))
    out = pl.pallas_call(
        functools.partial(_cache_roll_body, wb=wb, t=t),
        grid=(n,),
        in_specs=[spec],
        out_specs=spec,
        out_shape=jax.ShapeDtypeStruct(flat.shape, F32),
        compiler_params=_params(1),
        name="cache_roll",
    )(flat)
    return out.reshape(cache_t.shape)


Q_PAD = 16


def _attn_sample_body(q_ref, kn_ref, vn_ref, c_ref, tail_ref, o_ref, lse_ref, tail_out, *,
                      wb, window, dil, t):
    heads = range(N_HEADS)
    tail_out[...] = tail_ref[...]
    tq = lax.broadcasted_iota(jnp.int32, (Q_PAD, wb), 0)
    w = lax.broadcasted_iota(jnp.int32, (Q_PAD, wb), 1)
    dist = wb + tq - w
    cache_ok = ((dist & (dil - 1)) == 0) & (dist <= window)
    tq = lax.broadcasted_iota(jnp.int32, (Q_PAD, LANE), 0)
    n = lax.broadcasted_iota(jnp.int32, (Q_PAD, LANE), 1)
    dist = tq - n
    new_ok = (n < t) & (dist >= 0) & ((dist & (dil - 1)) == 0)
    zrow = jnp.zeros((Q_PAD - t, HEAD_DIM), F32)
    zcol = jnp.zeros((HEAD_DIM, LANE - t), F32)
    q = [jnp.concatenate([q_ref[0, 0, h], zrow], axis=0).astype(BF16) for h in heads]
    kn = [jnp.concatenate([kn_ref[0, 0, h], zcol], axis=1).astype(BF16) for h in heads]
    vn = [jnp.concatenate([vn_ref[0, 0, h], zcol], axis=1).astype(BF16) for h in heads]
    s_c = [jnp.where(cache_ok, _dot(q[h], c_ref[0, 0, 0, h].astype(BF16)), NEG_INF) for h in heads]
    s_n = [jnp.where(new_ok, _dot(q[h], kn[h]), NEG_INF) for h in heads]
    mx = [jnp.maximum(jnp.max(s_c[h], axis=-1, keepdims=True),
                      jnp.max(s_n[h], axis=-1, keepdims=True)) for h in heads]
    p_c = [jnp.exp(s_c[h] - mx[h]) for h in heads]
    p_n = [jnp.exp(s_n[h] - mx[h]) for h in heads]
    den = [jnp.sum(p_c[h], axis=-1, keepdims=True) + jnp.sum(p_n[h], axis=-1, keepdims=True)
           for h in heads]
    acc = [_dot_nt(p_c[h].astype(BF16), c_ref[0, 0, 1, h].astype(BF16))
           + _dot_nt(p_n[h].astype(BF16), vn[h]) for h in heads]
    for h in heads:
        o_ref[0, h] = (acc[h] / den[h])[0:t]
        lse_ref[0, h] = jnp.broadcast_to((mx[h] + jnp.log(den[h]))[0:t], (t, HEAD_DIM))
        tail_out[0, 0, 0, h, :, LANE - t:LANE] = kn_ref[0, 0, h]
        tail_out[0, 0, 1, h, :, LANE - t:LANE] = vn_ref[0, 0, h]


def _attn_sample(q_hm, kv_t, cache_t, stacked, layer, b, t, g):
    window, dil = ATT_GROUPS[g]
    wb = cache_t.shape[-1]
    assert wb == window and wb % LANE == 0 and t <= Q_PAD
    mat = (N_HEADS, HEAD_DIM)
    qspec = pl.BlockSpec((1, 1, N_HEADS, t, HEAD_DIM), lambda i: (i, g, 0, 0, 0))
    nspec = lambda part: pl.BlockSpec((1, 1) + mat + (t,), lambda i: (i, part * N_GROUPS + g, 0, 0, 0))
    cspec = pl.BlockSpec((1, 1, 2) + mat + (wb,), lambda i: (layer, i, 0, 0, 0, 0))
    tspec = pl.BlockSpec((1, 1, 2) + mat + (LANE,), lambda i: (layer, i, 0, 0, 0, wb // LANE - 1))
    ospec = pl.BlockSpec((1, N_HEADS, t, HEAD_DIM), lambda i: (i, 0, 0, 0))
    oshape = jax.ShapeDtypeStruct((b, N_HEADS, t, HEAD_DIM), F32)
    return pl.pallas_call(
        functools.partial(_attn_sample_body, wb=wb, window=window, dil=dil, t=t),
        grid=(b,),
        in_specs=[qspec, nspec(1), nspec(2), cspec, tspec],
        out_specs=[ospec, ospec, tspec],
        out_shape=[oshape, oshape, jax.ShapeDtypeStruct(cache_t.shape, F32)],
        input_output_aliases={4: 2},
        compiler_params=_params(1),
        name=f"attn_sample_g{g}",
    )(q_hm, kv_t, kv_t, cache_t, stacked)


def _merge_body(x_ref, pg_ref, y_ref, r_ref, k_ref, v_ref, g_ref, o0_ref, o1_ref, o2_ref,
                l0_ref, l1_ref, l2_ref, wa_ref, wb_ref, wo_ref, gng_ref, gnb_ref, rk_ref,
                lng_ref, lnb_ref, ones_ref, h_ref, *, alpha):
    ones_bd = ones_ref[...]
    y = y_ref[...]
    v = v_ref[...]
    mu = _head_sum(y, ones_bd) * (1.0 / HEAD_DIM)
    d = y - mu
    var = _head_sum(d * d, ones_bd) * (1.0 / HEAD_DIM)
    yn = d * lax.rsqrt(var + GN_EPS) * gng_ref[...] + gnb_ref[...]
    bonus = _head_sum(r_ref[...] * k_ref[...] * rk_ref[...], ones_bd) * v
    rw = (yn + bonus) * g_ref[...]
    l0, l1, l2 = l0_ref[...], l1_ref[...], l2_ref[...]
    mx = jnp.maximum(jnp.maximum(l0, l1), l2)
    e0, e1, e2 = jnp.exp(l0 - mx), jnp.exp(l1 - mx), jnp.exp(l2 - mx)
    att = (e0 * o0_ref[...] + e1 * o1_ref[...] + e2 * o2_ref[...]) / (e0 + e1 + e2)
    pg = pg_ref[...].astype(F32)
    merged = (_sigmoid(pg[:, :D_MODEL]) * _dot(rw.astype(BF16), wa_ref[...])
              + _sigmoid(pg[:, D_MODEL:]) * _dot(att.astype(BF16), wb_ref[...]))
    pre = alpha * x_ref[...] + _dot(merged.astype(BF16), wo_ref[...])
    h_ref[...] = _layer_norm(pre, lng_ref[...], lnb_ref[...])


def _merge(x, p_gate, y, r, k, v, g, outs, lses, lw, tm, alpha):
    m = x.shape[0]
    row = lambda c: pl.BlockSpec((tm, c), lambda i: (i, 0))
    const = lambda a: pl.BlockSpec(a.shape, lambda i: (0,) * a.ndim)
    consts = [lw["w_br_a"], lw["w_br_b"], lw["w_out"], lw["gn_g"], lw["gn_b"], lw["rk"],
              lw["ln1_g"], lw["ln1_b"], lw["ones_bd"]]
    return pl.pallas_call(
        functools.partial(_merge_body, alpha=alpha),
        grid=(m // tm,),
        in_specs=[row(D_MODEL), row(C_GATE)] + [row(WIDTH)] * 11 + [const(a) for a in consts],
        out_specs=row(D_MODEL),
        out_shape=jax.ShapeDtypeStruct((m, D_MODEL), F32),
        compiler_params=_params(1),
        name="merge_ln1",
    )(x, p_gate, y, r, k, v, g, *outs, *lses, *consts)


def _gelu(x):
    return 0.5 * x * (1.0 + lax.erf(x * (2.0 ** -0.5)))


def _ffn_body(*refs, seq, tm, nf, carry_mode, alpha):
    if carry_mode:
        (h_ref, wu_ref, wg_ref, wd_ref, cw_ref, cb_ref, lng_ref, lnb_ref, buf_ref,
         y_ref, cnew_ref, hb_ref, act_ref, carry_ref) = refs
    else:
        (h_ref, wu_ref, wg_ref, wd_ref, cw_ref, cb_ref, lng_ref, lnb_ref, e1_ref, e2_ref,
         y_ref, u_ref, hb_ref, act_ref) = refs
    i = pl.program_id(0)
    j = pl.program_id(1)

    @pl.when(j == 0)
    def _():
        hb_ref[...] = h_ref[...].astype(BF16)

    hb = hb_ref[...]
    u = _dot(hb, wu_ref[...])
    gate = _dot(hb, wg_ref[...])
    row = lax.broadcasted_iota(jnp.int32, u.shape, 0)
    r1 = pltpu.roll(u, 1, 0)
    r2 = pltpu.roll(u, 2, 0)
    if carry_mode:
        slot = i % 2

        @pl.when(i % (seq // tm) == 0)
        def _():
            carry_ref[1 - slot, j, 6:8, :] = buf_ref[0]

        tail = carry_ref[1 - slot, j]
        u1 = jnp.where(row == 0, tail[7:8, :], r1)
        u2 = jnp.where(row == 0, tail[6:7, :], jnp.where(row == 1, tail[7:8, :], r2))
        carry_ref[slot, j] = u[tm - 8:tm, :]
        cnew_ref[0] = u[tm - 2:tm, :]
    else:
        u1 = jnp.where(row % seq == 0, e1_ref[...], r1)
        u2 = jnp.where(row % seq < 2, e2_ref[...], r2)
        u_ref[...] = u
    cw = cw_ref[...]
    conv = cb_ref[...] + cw[0:1, :] * u2 + cw[1:2, :] * u1 + cw[2:3, :] * u
    act_ref[j] = (_gelu(conv) * gate).astype(BF16)

    @pl.when(j == nf - 1)
    def _():
        down = _dot(act_ref[0], wd_ref[0])
        for c in range(1, nf):
            down = down + _dot(act_ref[c], wd_ref[c])
        y_ref[...] = _layer_norm(alpha * h_ref[...] + down, lng_ref[...], lnb_ref[...])


def _conv_ffn(h, conv_buf, lw, seq, tm, carry_mode, alpha):
    m = h.shape[0]
    b = m // seq
    nf = D_FF // FF_CHUNK
    tf = FF_CHUNK
    const = lambda a: pl.BlockSpec(a.shape, lambda i, j: (0,) * a.ndim)
    in_specs = [pl.BlockSpec((tm, D_MODEL), lambda i, j: (i, 0)),
                pl.BlockSpec((D_MODEL, tf), lambda i, j: (0, j)),
                pl.BlockSpec((D_MODEL, tf), lambda i, j: (0, nf + j)),
                pl.BlockSpec((nf, tf, D_MODEL), lambda i, j: (0, 0, 0)),
                pl.BlockSpec((CONV_W, tf), lambda i, j: (0, j)),
                pl.BlockSpec((1, tf), lambda i, j: (0, j)),
                const(lw["ln2_g"]), const(lw["ln2_b"])]
    args = [h, lw["w_up"], lw["w_up"], lw["w_down"].reshape(nf, tf, D_MODEL), lw["conv_w"],
            lw["conv_b"], lw["ln2_g"], lw["ln2_b"]]
    scratch = [pltpu.VMEM((tm, D_MODEL), BF16), pltpu.VMEM((nf, tm, tf), BF16)]
    y_spec = pl.BlockSpec((tm, D_MODEL), lambda i, j: (i, 0))
    y_shape = jax.ShapeDtypeStruct((m, D_MODEL), F32)
    if carry_mode:
        tps = seq // tm
        in_specs.append(pl.BlockSpec((1, CONV_W - 1, tf), lambda i, j: (i // tps, 0, j)))
        args.append(conv_buf)
        out_specs = [y_spec, pl.BlockSpec((1, CONV_W - 1, tf), lambda i, j: (i, 0, j))]
        out_shape = [y_shape, jax.ShapeDtypeStruct((m // tm, CONV_W - 1, D_FF), F32)]
        scratch.append(pltpu.VMEM((2, nf, 8, tf), F32))
    else:
        zeros = jnp.zeros((b, seq, D_FF), F32)
        e1 = zeros.at[:, 0].set(conv_buf[:, 1]).reshape(m, D_FF)
        e2 = zeros.at[:, 0].set(conv_buf[:, 0]).at[:, 1].set(conv_buf[:, 1]).reshape(m, D_FF)
        tile = pl.BlockSpec((tm, tf), lambda i, j: (i, j))
        in_specs += [tile, tile]
        args += [e1, e2]
        out_specs = [y_spec, tile]
        out_shape = [y_shape, jax.ShapeDtypeStruct((m, D_FF), F32)]
    y, aux = pl.pallas_call(
        functools.partial(_ffn_body, seq=seq, tm=tm, nf=nf, carry_mode=carry_mode, alpha=alpha),
        grid=(m // tm, nf),
        in_specs=in_specs,
        out_specs=out_specs,
        out_shape=out_shape,
        scratch_shapes=scratch,
        compiler_params=_params(2),
        name="conv_ffn_ln2",
    )(*args)
    if carry_mode:
        return y, aux[seq // tm - 1::seq // tm]
    return y, aux.reshape(b, seq, D_FF)[:, seq - (CONV_W - 1):]


def _rope_tables(pos):
    half = HEAD_DIM // 2
    inv = ROPE_THETA ** (-jnp.arange(half, dtype=F32) / half)
    ang = pos.astype(F32)[:, None] * inv[None, :]
    cos, sin = jnp.cos(ang), jnp.sin(ang)
    cos = jnp.tile(jnp.concatenate([cos, cos], axis=-1), (1, N_HEADS))
    sin = jnp.tile(jnp.concatenate([-sin, sin], axis=-1), (1, N_HEADS))
    return cos, sin


def _pad_cols(a, n):
    return jnp.pad(a, ((0, 0), (0, n - a.shape[1])))


def _layer_weights(l, w):
    row = lambda a: a.reshape(1, -1)
    z = lambda r, c: jnp.zeros((r, c), F32)
    w_in = w["w_in"][l]
    if l == 0:
        vres, mu_v = z(D_MODEL, D_MV_LORA), z(1, D_MV_LORA)
        v0, v2 = z(1, WIDTH), z(D_MV_LORA, WIDTH)
    else:
        vres, mu_v = w["w_in_vres"][l - 1], row(w["mu_vres"][l - 1])
        v0, v2 = row(w["rw_v0"][l - 1]), w["rw_v2"][l - 1]
    w_rw = _pad_cols(jnp.concatenate([w_in[:, C_ATT + C_GATE:], vres], axis=1), C_RW_PAD)
    mu = _pad_cols(jnp.concatenate([row(w["mu_rw"][l]), mu_v], axis=1), C_RW_PAD)
    w2a2 = jnp.concatenate([
        jnp.concatenate([w["rw_w2"][l], z(D_DECAY_LORA, WIDTH)], axis=1),
        jnp.concatenate([z(D_AAA_LORA, WIDTH), w["rw_a2"][l]], axis=1)], axis=0)
    pad_rows = 256 - D_GATE_LORA - D_MV_LORA
    g2v2 = jnp.concatenate([
        jnp.concatenate([w["rw_g2"][l], z(D_GATE_LORA, WIDTH)], axis=1),
        jnp.concatenate([z(D_MV_LORA, WIDTH), v2], axis=1),
        z(pad_rows, 2 * WIDTH)], axis=0)
    head = jnp.arange(WIDTH) // HEAD_DIM
    return {
        "w_att": w_in[:, :C_ATT].astype(BF16),
        "w_gate": w_in[:, C_ATT:C_ATT + C_GATE].astype(BF16),
        "w_rw": w_rw.astype(BF16),
        "mu": mu, "w0": row(w["rw_w0"][l]), "a0": row(w["rw_a0"][l]), "v0": v0,
        "kk": row(w["rw_kk"][l]), "ka": row(w["rw_ka"][l]),
        "w2a2": w2a2.astype(BF16), "g2v2": g2v2.astype(BF16),
        "ones_bd": (head[:, None] == head[None, :]).astype(BF16),
        "is_l0": jnp.full((1, WIDTH), 1.0 if l == 0 else 0.0, F32),
        "gn_g": row(w["rw_gn_g"][l]), "gn_b": row(w["rw_gn_b"][l]), "rk": row(w["rw_rk"][l]),
        "w_br_a": w["w_br_a"][l].astype(BF16), "w_br_b": w["w_br_b"][l].astype(BF16),
        "w_out": w["w_out"][l].astype(BF16),
        "ln1_g": row(w["ln1_g"][l]), "ln1_b": row(w["ln1_b"][l]),
        "w_up": w["ffn_w_up"][l].astype(BF16), "w_down": w["ffn_w_down"][l].astype(BF16),
        "conv_w": w["ffn_conv_w"][l], "conv_b": row(w["ffn_conv_b"][l]),
        "ln2_g": row(w["ln2_g"][l]), "ln2_b": row(w["ln2_b"][l]),
    }


def _trunk_layer(x, b, t, x_prev, wkv0, caches, conv_buf, v_first, lw, cos, sin, alpha, layer=0,
                 stacked=None):
    m = b * t
    prompt = caches is None
    tm = min(m, 1024 if prompt else 256)
    act_dtype = BF16 if prompt else F32
    p_att = _mm(x, lw["w_att"], tm, C_ATT // 3, act_dtype)
    p_gate = _mm(x, lw["w_gate"], tm, C_GATE, act_dtype)
    p_rw = _mm(x, lw["w_rw"], tm, C_RW_PAD)
    bp = -(-b // 8) * 8
    prev0 = _mm(jnp.pad(x_prev, ((0, bp - b), (0, 0))), lw["w_rw"], bp, C_RW_PAD // 3)[:b]

    if not prompt:
        p_nat = _rope_sample(p_att, jnp.tile(cos, (b, 1)), jnp.tile(sin, (b, 1)))
        p_nat = p_nat.reshape(b, t, 3 * N_GROUPS, N_HEADS, HEAD_DIM)
        q_hm = p_nat.transpose(0, 2, 3, 1, 4)
        kv_t = p_nat.transpose(0, 2, 3, 4, 1)
        flat = lambda a: a.transpose(0, 2, 1, 3).reshape(m, WIDTH)
    outs, lses, wins = [], [], []
    for g, (window, dil) in enumerate(ATT_GROUPS):
        if prompt:
            o, lse, kwin = _attn_prompt(p_att, cos, sin, b, t, g)
            lo = (2 * N_GROUPS + g) * WIDTH
            vwin = p_att.reshape(b, t, C_ATT)[:, t - window:, lo:lo + WIDTH].astype(F32)
            win = jnp.stack([kwin, vwin], axis=1).reshape(b, 2, window, N_HEADS, HEAD_DIM)
        else:
            o, lse, win = _attn_sample(q_hm, kv_t, caches[g], stacked[g], layer, b, t, g)
            o, lse = flat(o), flat(lse)
        outs.append(o)
        lses.append(lse)
        wins.append(win)

    carry_mode = t >= 512
    tm_rw = 512 if carry_mode else m
    if carry_mode:
        prev_in = prev0.reshape(b, 1, C_RW_PAD)
    else:
        prev_in = jnp.repeat(prev0, t, axis=0)
    vf_in = jnp.zeros((m, WIDTH), F32) if v_first is None else v_first
    r, lgw, k, v, kk, a, g_out = _time_mix(p_rw, prev_in, vf_in, lw, t, tm_rw, carry_mode)
    if v_first is None:
        v_first = v

    tp = -(-t // WKV_CHUNK) * WKV_CHUNK
    tok = lambda a_: jnp.pad(a_.reshape(b, t, WIDTH), ((0, 0), (0, tp - t), (0, 0)))
    y, wkv_new = _wkv(tok(r), tok(lgw), tok(k), tok(v), tok(kk), tok(a), wkv0)
    y = y[:, :t].reshape(m, WIDTH)

    tm_mg = min(m, 256)
    h = _merge(x, p_gate, y, r, k, v, g_out, outs, lses, lw, tm_mg, alpha)
    y_out, conv_new = _conv_ffn(h, conv_buf, lw, t, 1024 if carry_mode else m, carry_mode, alpha)
    shift = x.reshape(b, t, D_MODEL)[:, -1]
    return y_out, wins, wkv_new, shift, conv_new, v_first


def kernel(x_prompt, x_sample, cache_win128, cache_win512, cache_win2048, state_wkv, state_shift, state_ffn_conv, w_in, w_in_vres, mu_rw, mu_vres, rw_w0, rw_w2, rw_a0, rw_a2, rw_g2, rw_v0, rw_v2, rw_kk, rw_ka, rw_rk, rw_gn_g, rw_gn_b, w_br_a, w_br_b, w_out, ln1_g, ln1_b, ffn_w_up, ffn_conv_w, ffn_conv_b, ffn_w_down, ln2_g, ln2_b):
    w = dict(w_in=w_in, w_in_vres=w_in_vres, mu_rw=mu_rw, mu_vres=mu_vres, rw_w0=rw_w0, rw_w2=rw_w2,
             rw_a0=rw_a0, rw_a2=rw_a2, rw_g2=rw_g2, rw_v0=rw_v0, rw_v2=rw_v2, rw_kk=rw_kk,
             rw_ka=rw_ka, rw_rk=rw_rk, rw_gn_g=rw_gn_g, rw_gn_b=rw_gn_b, w_br_a=w_br_a,
             w_br_b=w_br_b, w_out=w_out, ln1_g=ln1_g, ln1_b=ln1_b, ffn_w_up=ffn_w_up,
             ffn_conv_w=ffn_conv_w, ffn_conv_b=ffn_conv_b, ffn_w_down=ffn_w_down,
             ln2_g=ln2_g, ln2_b=ln2_b)
    caches = tuple(c.transpose(0, 1, 2, 4, 5, 3) for c in (cache_win128, cache_win512, cache_win2048))
    depth = w_in.shape[0]
    alpha = ALPHA
    bp, sp, _ = x_prompt.shape
    bs, ts, _ = x_sample.shape
    cos_p, sin_p = _rope_tables(jnp.arange(sp, dtype=jnp.int32))
    cos_s, sin_s = _rope_tables(PAST_LEN + jnp.arange(ts, dtype=jnp.int32))
    xp = x_prompt.reshape(bp * sp, D_MODEL)
    xs = x_sample.reshape(bs * ts, D_MODEL)
    vf_p = vf_s = None
    win_p = [[] for _ in ATT_GROUPS]
    win_s = [_cache_roll(c, ts) for c in caches]
    wkv_p, wkv_s, sh_p, sh_s, cv_p, cv_s = [], [], [], [], [], []
    for l in range(depth):
        lw = _layer_weights(l, w)
        xp, nw, s_new, sh, cv, vf_p = _trunk_layer(
            xp, bp, sp, jnp.zeros((bp, D_MODEL), F32),
            jnp.zeros((bp, N_HEADS, HEAD_DIM, HEAD_DIM), F32), None,
            jnp.zeros((bp, CONV_W - 1, D_FF), F32), vf_p, lw, cos_p, sin_p, alpha)
        for g in range(N_GROUPS):
            win_p[g].append(nw[g])
        wkv_p.append(s_new)
        sh_p.append(sh)
        cv_p.append(cv)
        xs, win_s, s_new, sh, cv, vf_s = _trunk_layer(
            xs, bs, ts, state_shift[l], state_wkv[l], caches,
            state_ffn_conv[l], vf_s, lw, cos_s, sin_s, alpha, l, win_s)
        wkv_s.append(s_new)
        sh_s.append(sh)
        cv_s.append(cv)
    return (xp.reshape(bp, sp, D_MODEL), xs.reshape(bs, ts, D_MODEL),
            jnp.stack(win_p[0]), win_s[0].transpose(0, 1, 2, 5, 3, 4),
            jnp.stack(win_p[1]), win_s[1].transpose(0, 1, 2, 5, 3, 4),
            jnp.stack(win_p[2]), win_s[2].transpose(0, 1, 2, 5, 3, 4),
            jnp.stack(wkv_p), jnp.stack(wkv_s),
            jnp.stack(sh_p), jnp.stack(sh_s), jnp.stack(cv_p), jnp.stack(cv_s))
```

```python
import functools

import jax
import jax.numpy as jnp
from jax import lax
from jax.experimental import pallas as pl
from jax.experimental.pallas import tpu as pltpu

F32 = jnp.float32
BF16 = jnp.bfloat16

D_MODEL = 1024
HEAD_DIM = 64
N_HEADS = 8
WIDTH = N_HEADS * HEAD_DIM
ATT_GROUPS = ((128, 1), (512, 4), (2048, 16))
N_GROUPS = len(ATT_GROUPS)
BLK = 128
ROPE_THETA = 10000.0
D_DECAY_LORA = 64
D_AAA_LORA = 64
D_GATE_LORA = 160
D_MV_LORA = 32
D_FF = 2816
CONV_W = 3
LN_EPS = 1e-5
GN_EPS = 64e-5
DEPTH = 4
ALPHA = (2.0 * DEPTH) ** 0.25
PAST_LEN = 8192
C_ATT = 3 * N_GROUPS * WIDTH
C_GATE = 2 * D_MODEL
C_RW = 3 * WIDTH + D_DECAY_LORA + D_AAA_LORA + D_GATE_LORA
C_RW_PAD = 1920
LORA_WA = 3 * WIDTH
LORA_GV = LORA_WA + 128
FF_CHUNK = 256
WKV_CHUNK = 64
ATT_ROWS = 2048
ATT_GROUP = 8
LANE = 128
VMEM_LIMIT = 56 * 1024 * 1024
NEG_INF = float("-inf")


def _params(n_axes):
    return pltpu.CompilerParams(
        dimension_semantics=("arbitrary",) * n_axes, vmem_limit_bytes=VMEM_LIMIT)


def _dot(a, b):
    return jnp.dot(a, b, preferred_element_type=F32)


NN = ((1,), (0,))
NT = ((1,), (1,))
TN = ((0,), (0,))


def _dg(a, b, dims):
    return lax.dot_general(a, b, (dims, ((), ())), preferred_element_type=F32)


def _dot_nt(a, b):
    return _dg(a, b, NT)


def _split(x):
    hi = x.astype(BF16)
    return hi, (x - hi.astype(F32)).astype(BF16)


def _dot_x1(a, b, dims=NN):
    return _dg(a.astype(BF16), b.astype(BF16), dims)


def _head_sum(x, ones_bd):
    hi, lo = _split(x)
    return _dot(hi, ones_bd) + _dot(lo, ones_bd)


def _layer_norm(x, g, b):
    mu = jnp.mean(x, axis=-1, keepdims=True)
    d = x - mu
    var = jnp.mean(d * d, axis=-1, keepdims=True)
    return d * lax.rsqrt(var + LN_EPS) * g + b


def _sigmoid(x):
    return 1.0 / (1.0 + jnp.exp(-x))


def _rope(t, cos, sin):
    half = HEAD_DIM // 2
    outs = []
    for c in range(t.shape[-1] // LANE):
        tc = t[:, c * LANE:(c + 1) * LANE]
        lane = lax.broadcasted_iota(jnp.int32, tc.shape, 1)
        fwd = pltpu.roll(tc, LANE - half, 1)
        bwd = pltpu.roll(tc, half, 1)
        outs.append(jnp.where((lane & (HEAD_DIM - 1)) < half, fwd, bwd))
    partner = jnp.concatenate(outs, axis=-1)
    return t * cos + partner * sin


def _mm_body(x_ref, w_ref, o_ref, xb_ref):
    @pl.when(pl.program_id(1) == 0)
    def _():
        xb_ref[...] = x_ref[...].astype(BF16)

    o_ref[...] = _dot(xb_ref[...], w_ref[...]).astype(o_ref.dtype)


def _mm(x, w, tm, tn, out_dtype=F32):
    m, k = x.shape
    n = w.shape[1]
    return pl.pallas_call(
        _mm_body,
        grid=(m // tm, n // tn),
        in_specs=[pl.BlockSpec((tm, k), lambda i, j: (i, 0)),
                  pl.BlockSpec((k, tn), lambda i, j: (0, j))],
        out_specs=pl.BlockSpec((tm, tn), lambda i, j: (i, j)),
        out_shape=jax.ShapeDtypeStruct((m, n), out_dtype),
        scratch_shapes=[pltpu.VMEM((tm, k), BF16)],
        compiler_params=_params(2),
        name="proj_mm",
    )(x, w)


def _tm_body(*refs, seq, tm, carry_mode):
    if carry_mode:
        (p_ref, prev0_ref, mu_ref, w0_ref, a0_ref, v0_ref, kkp_ref, kap_ref, w2a2_ref, g2v2_ref,
         ones_ref, vf_ref, l0_ref,
         r_out, lw_out, k_out, v_out, kk_out, a_out, g_out, carry_ref) = refs
    else:
        (p_ref, prev0_ref, mu_ref, w0_ref, a0_ref, v0_ref, kkp_ref, kap_ref, w2a2_ref, g2v2_ref,
         ones_ref, vf_ref, l0_ref,
         r_out, lw_out, k_out, v_out, kk_out, a_out, g_out) = refs
    p = p_ref[...]
    row = lax.broadcasted_iota(jnp.int32, p.shape, 0)
    rolled = pltpu.roll(p, 1, 0)
    if carry_mode:
        slot = pl.program_id(0) % 2

        @pl.when(pl.program_id(0) % (seq // tm) == 0)
        def _():
            carry_ref[1 - slot, 0:1, :] = prev0_ref[0]

        p_prev = jnp.where(row == 0, carry_ref[1 - slot, 0:1, :], rolled)
        carry_ref[slot, 0:1, :] = p[tm - 1:tm, :]
    else:
        p_prev = jnp.where(row % seq == 0, prev0_ref[...], rolled)
    z = p + (p_prev - p) * mu_ref[...]
    r = z[:, :WIDTH]
    k = z[:, WIDTH:2 * WIDTH]
    v = z[:, 2 * WIDTH:3 * WIDTH]
    wa = z[:, LORA_WA:LORA_WA + 128]
    lane = lax.broadcasted_iota(jnp.int32, wa.shape, 1)
    wa = jnp.where(lane < D_DECAY_LORA, jnp.tanh(wa), wa)
    wa_out = _dot(wa.astype(BF16), w2a2_ref[...])
    gv = z[:, LORA_GV:LORA_GV + 256]
    lane = lax.broadcasted_iota(jnp.int32, gv.shape, 1)
    gv = jnp.where(lane < D_GATE_LORA, _sigmoid(gv), gv)
    gv_out = _dot(gv.astype(BF16), g2v2_ref[...])
    x = -(w0_ref[...] + wa_out[:, :WIDTH])
    softplus = jnp.maximum(x, 0.0) + jnp.log1p(jnp.exp(-jnp.abs(x)))
    lw_out[...] = -jnp.exp(-softplus - 0.5)
    a = _sigmoid(a0_ref[...] + wa_out[:, WIDTH:])
    g_out[...] = gv_out[:, :WIDTH]
    vf = jnp.where(l0_ref[...] > 0.5, v, vf_ref[...])
    v = v + (vf - v) * _sigmoid(v0_ref[...] + gv_out[:, WIDTH:])
    kk = k * kkp_ref[...]
    norm = jnp.sqrt(_head_sum(kk * kk, ones_ref[...]))
    kk_out[...] = kk / jnp.maximum(norm, 1e-12)
    k_out[...] = k * (1.0 + (a - 1.0) * kap_ref[...])
    r_out[...] = r
    v_out[...] = v
    a_out[...] = a


def _time_mix(p_rw, prev0, vfirst, lw, seq, tm, carry_mode):
    m = p_rw.shape[0]
    row = lambda c: pl.BlockSpec((tm, c), lambda i: (i, 0))
    const = lambda a: pl.BlockSpec(a.shape, lambda i: (0,) * a.ndim)
    if carry_mode:
        prev_spec = pl.BlockSpec((1, 1, C_RW_PAD), lambda i: (i // (seq // tm), 0, 0))
        scratch = [pltpu.VMEM((2, 8, C_RW_PAD), F32)]
    else:
        prev_spec = row(C_RW_PAD)
        scratch = []
    consts = [lw["mu"], lw["w0"], lw["a0"], lw["v0"], lw["kk"], lw["ka"], lw["w2a2"], lw["g2v2"],
              lw["ones_bd"]]
    out = jax.ShapeDtypeStruct((m, WIDTH), F32)
    return pl.pallas_call(
        functools.partial(_tm_body, seq=seq, tm=tm, carry_mode=carry_mode),
        grid=(m // tm,),
        in_specs=[row(C_RW_PAD), prev_spec] + [const(a) for a in consts]
        + [row(WIDTH), const(lw["is_l0"])],
        out_specs=[row(WIDTH)] * 7,
        out_shape=[out] * 7,
        scratch_shapes=scratch,
        compiler_params=_params(1),
        name="time_mix",
    )(p_rw, prev0, *consts, vfirst, lw["is_l0"])


def _wkv_body(r_ref, lw_ref, k_ref, v_ref, kk_ref, a_ref, s0_ref, y_ref, sout_ref, s_ref, *,
              nc, nb):
    c = pl.program_id(1)
    n = WKV_CHUNK

    @pl.when(c == 0)
    def _():
        s_ref[...] = s0_ref[...]

    row = lax.broadcasted_iota(jnp.int32, (n, n), 0)
    col = lax.broadcasted_iota(jnp.int32, (n, n), 1)
    incl = col <= row
    strict = col < row
    eye = (col == row).astype(F32)
    row2 = lax.broadcasted_iota(jnp.int32, (n, 2 * n), 0)
    col2 = lax.broadcasted_iota(jnp.int32, (n, 2 * n), 1)
    incl_cat = (col2 & (n - 1)) <= row2
    tri = incl.astype(BF16)
    where, ar, bk, vh, bk_end, g_end = [], [], [], [], [], []
    for bi in range(nb):
        lw = lw_ref[bi]
        lw_hi = lw.astype(BF16)
        lw_mid, lw_lo = _split(lw - lw_hi.astype(F32))
        cum = _dot(tri, lw_hi) + _dot(tri, lw_mid) + _dot(tri, lw_lo)
        cum_end = cum[n - 1:n, :]
        kk = kk_ref[bi]
        k = k_ref[bi]
        kka = kk * a_ref[bi]
        e_neg = jnp.exp(-cum)
        e_end = jnp.exp(cum_end - cum)
        a_t = -kk * jnp.exp(cum - lw)
        b_t = kka * e_neg
        k_t = k * e_neg
        r_t = r_ref[bi] * jnp.exp(cum)
        b_h = kka * e_end
        k_h = k * e_end
        decay_end = jnp.exp(cum_end)
        v = v_ref[bi]
        for h in range(N_HEADS):
            sl = slice(h * HEAD_DIM, (h + 1) * HEAD_DIM)
            where.append((bi, h, sl))
            ar.append(jnp.concatenate([a_t[:, sl], r_t[:, sl]], axis=0))
            bk.append(jnp.concatenate([b_t[:, sl], k_t[:, sl]], axis=0))
            bk_end.append(jnp.concatenate([b_h[:, sl], k_h[:, sl]], axis=0))
            vh.append(v[:, sl])
            g_end.append(decay_end[:, sl])
    heads = range(len(where))
    s = [s_ref[bi, h] for bi, h, _ in where]
    gram = [_dot_x1(ar[h], bk[h], NT) for h in heads]
    on_s = [_dot_x1(ar[h], s[h], NT) for h in heads]
    l_ab = [jnp.where(strict, g[:n, :n], 0.0) for g in gram]
    l_ak = [jnp.where(strict, g[:n, n:], 0.0) for g in gram]
    m_cat = [jnp.where(incl_cat, g[n:, :], 0.0) for g in gram]
    rhs = [on_s[h][:n] + _dot_x1(l_ak[h], vh[h]) for h in heads]
    acc = [eye + l for l in l_ab]
    pw = [_dot_x1(l, l) for l in l_ab]
    span = 2
    while span * 2 < n:
        both = [_dot_x1(jnp.concatenate([acc[h], pw[h]], axis=0), pw[h]) for h in heads]
        acc = [acc[h] + both[h][:n] for h in heads]
        pw = [both[h][n:] for h in heads]
        span *= 2
    acc = [acc[h] + _dot_x1(acc[h], pw[h]) for h in heads]
    u = [_dot_x1(acc[h], rhs[h]) for h in heads]
    uv = [jnp.concatenate([u[h], vh[h]], axis=0) for h in heads]
    for h, (bi, _, sl) in enumerate(where):
        y_ref[bi, :, sl] = on_s[h][n:] + _dot_x1(m_cat[h], uv[h])
    for h, (bi, hd, _) in enumerate(where):
        s_ref[bi, hd] = s[h] * g_end[h] + _dot_x1(uv[h], bk_end[h], TN)

    @pl.when(c == nc - 1)
    def _():
        sout_ref[...] = s_ref[...]


def _wkv(r, lw, k, v, kk, a, s0):
    b, t, _ = r.shape
    nc = t // WKV_CHUNK
    nb = 2 if b % 2 == 0 else 1
    tok = pl.BlockSpec((nb, WKV_CHUNK, WIDTH), lambda i, c: (i, c, 0))
    st = pl.BlockSpec((nb, N_HEADS, HEAD_DIM, HEAD_DIM), lambda i, c: (i, 0, 0, 0))
    return pl.pallas_call(
        functools.partial(_wkv_body, nc=nc, nb=nb),
        grid=(b // nb, nc),
        in_specs=[tok] * 6 + [st],
        out_specs=[tok, st],
        out_shape=[jax.ShapeDtypeStruct((b, t, WIDTH), F32),
                   jax.ShapeDtypeStruct((b, N_HEADS, HEAD_DIM, HEAD_DIM), F32)],
        scratch_shapes=[pltpu.VMEM((nb, N_HEADS, HEAD_DIM, HEAD_DIM), F32)],
        compiler_params=_params(2),
        name="wkv_chunked",
    )(r, lw, k, v, kk, a, s0)


def _attn_prompt_body(q_ref, k_ref, v_ref, cos_ref, sin_ref, o_ref, lse_ref, kwin_ref,
                      qs_ref, ks_ref, vs_ref, kc_ref, vc_ref, *, nblk, dil, window):
    j = pl.program_id(2)
    slot = j % 2
    cos = cos_ref[...]
    sin = sin_ref[...]
    qs_ref[...] = _rope(q_ref[0].astype(F32), cos, sin) * (HEAD_DIM ** -0.5)
    k_rot = _rope(k_ref[0].astype(F32), cos, sin)
    ks_ref[...] = k_rot
    vs_ref[...] = v_ref[0].astype(F32)

    @pl.when(j == nblk - 1)
    def _():
        kwin_ref[0] = k_rot[ATT_ROWS - window:, :]

    @pl.when(j == 0)
    def _():
        kc_ref[1] = jnp.zeros((dil, BLK, LANE), BF16)
        vc_ref[1] = jnp.zeros((dil, BLK, LANE), BF16)

    lane = lax.broadcasted_iota(jnp.int32, (BLK, LANE), 1)
    first = lane < HEAD_DIM
    qi = lax.broadcasted_iota(jnp.int32, (BLK, BLK), 0)
    kj = lax.broadcasted_iota(jnp.int32, (BLK, BLK), 1)
    prev_ok = kj >= qi
    prev_ok_first = (kj - jnp.where(j > 0, 0, BLK)) >= qi
    cur_ok = kj <= qi
    mcount = ATT_ROWS // (BLK * dil)
    blocks = [(r, m) for r in range(dil) for m in range(mcount)]
    kcur, vcur = {}, {}
    for g0 in range(0, len(blocks), ATT_GROUP):
        group = blocks[g0:g0 + ATT_GROUP]
        rows, q, kp, vp, kb, vb, ok = [], [], [], [], [], [], []
        for r, m in group:
            start = r + dil * BLK * m
            rw = pl.ds(start, BLK, stride=dil) if dil > 1 else pl.ds(start, BLK)
            qf = qs_ref[rw, :]
            kcur[r, m] = ks_ref[rw, :].astype(BF16)
            vcur[r, m] = vs_ref[rw, :].astype(BF16)
            k_before = kc_ref[1 - slot, r] if m == 0 else kcur[r, m - 1]
            v_before = vc_ref[1 - slot, r] if m == 0 else vcur[r, m - 1]
            for q_head in (jnp.where(first, qf, 0.0), jnp.where(first, 0.0, qf)):
                q.append(q_head.astype(BF16))
                kp.append(k_before)
                vp.append(v_before)
                kb.append(kcur[r, m])
                vb.append(vcur[r, m])
                ok.append(prev_ok_first if m == 0 else prev_ok)
            rows.append(rw)
        chains = range(len(q))
        s_prev = [jnp.where(ok[c], _dot_nt(q[c], kp[c]), NEG_INF) for c in chains]
        s_cur = [jnp.where(cur_ok, _dot_nt(q[c], kb[c]), NEG_INF) for c in chains]
        mx = [jnp.maximum(jnp.max(s_prev[c], axis=-1, keepdims=True),
                          jnp.max(s_cur[c], axis=-1, keepdims=True)) for c in chains]
        p_prev = [jnp.exp(s_prev[c] - mx[c]) for c in chains]
        p_cur = [jnp.exp(s_cur[c] - mx[c]) for c in chains]
        den = [jnp.sum(p_prev[c], axis=-1, keepdims=True)
               + jnp.sum(p_cur[c], axis=-1, keepdims=True) for c in chains]
        acc = [_dot(p_prev[c].astype(BF16), vp[c]) + _dot(p_cur[c].astype(BF16), vb[c])
               for c in chains]
        lse = [mx[c] + jnp.log(den[c]) for c in chains]
        for idx, (r, m) in enumerate(group):
            c0, c1 = 2 * idx, 2 * idx + 1
            o_ref[0, rows[idx], :] = jnp.where(first, acc[c0] / den[c0], acc[c1] / den[c1])
            lse_ref[0, rows[idx], :] = jnp.where(first, lse[c0], lse[c1])
            if m == mcount - 1:
                kc_ref[slot, r] = kcur[r, m]
                vc_ref[slot, r] = vcur[r, m]


def _attn_prompt(p_att, cos, sin, b, s, g):
    window, dil = ATT_GROUPS[g]
    assert window // dil == BLK and window <= ATT_ROWS and s % ATT_ROWS == 0
    nblk = s // ATT_ROWS
    pairs = WIDTH // LANE
    pv = p_att.reshape(b, s, C_ATT)
    blk = lambda part: pl.BlockSpec(
        (1, ATT_ROWS, LANE), lambda i, hp, j: (i, j, (part * N_GROUPS + g) * pairs + hp))
    tspec = pl.BlockSpec((ATT_ROWS, LANE), lambda i, hp, j: (j, 0))
    ospec = pl.BlockSpec((1, ATT_ROWS, LANE), lambda i, hp, j: (i, j, hp))
    o, lse, kwin = pl.pallas_call(
        functools.partial(_attn_prompt_body, nblk=nblk, dil=dil, window=window),
        grid=(b, pairs, nblk),
        in_specs=[blk(0), blk(1), blk(2), tspec, tspec],
        out_specs=[ospec, ospec, pl.BlockSpec((1, window, LANE), lambda i, hp, j: (i, 0, hp))],
        out_shape=[jax.ShapeDtypeStruct((b, s, WIDTH), F32),
                   jax.ShapeDtypeStruct((b, s, WIDTH), F32),
                   jax.ShapeDtypeStruct((b, window, WIDTH), F32)],
        scratch_shapes=[pltpu.VMEM((ATT_ROWS, LANE), F32)] * 3 + [
                        pltpu.VMEM((2, dil, BLK, LANE), BF16), pltpu.VMEM((2, dil, BLK, LANE), BF16)],
        compiler_params=_params(3),
        name=f"attn_prompt_g{g}",
    )(pv, pv, pv, cos[:, :LANE], sin[:, :LANE])
    return o.reshape(b * s, WIDTH), lse.reshape(b * s, WIDTH), kwin


def _rope_sample_body(p_ref, cos_ref, sin_ref, o_ref):
    cos = cos_ref[...]
    sin = sin_ref[...]
    for part in range(3 * N_GROUPS):
        sl = slice(part * WIDTH, (part + 1) * WIDTH)
        x = p_ref[:, sl]
        if part < N_GROUPS:
            o_ref[:, sl] = _rope(x, cos, sin) * (HEAD_DIM ** -0.5)
        elif part < 2 * N_GROUPS:
            o_ref[:, sl] = _rope(x, cos, sin)
        else:
            o_ref[:, sl] = x


def _rope_sample(p_att, cos, sin):
    m = p_att.shape[0]
    full = lambda a: pl.BlockSpec(a.shape, lambda i: (0, 0))
    return pl.pallas_call(
        _rope_sample_body,
        grid=(1,),
        in_specs=[full(p_att), full(cos), full(sin)],
        out_specs=full(p_att),
        out_shape=jax.ShapeDtypeStruct((m, C_ATT), F32),
        compiler_params=_params(1),
        name="rope_sample",
    )(p_att, cos, sin)


def _cache_roll_body(c_ref, o_ref, *, wb, t):
    for kv in range(2):
        for h in range(N_HEADS):
            x = c_ref[0, kv, h]
            o_ref[0, kv, h] = jnp.concatenate([x[:, t:], x[:, :t]], axis=1)


def _cache_roll(cache_t, t):
    depth, b = cache_t.shape[:2]
    mat = cache_t.shape[2:]
    wb = mat[-1]
    n = depth * b
    flat = cache_t.reshape((n,) + mat)
    spec = pl.BlockSpec((1,) + mat, lambda i: (i, 0, 0, 0, 0))
    out = pl.pallas_call(
        functools.partial(_cache_roll_body, wb=wb, t=t),
        grid=(n,),
        in_specs=[spec],
        out_specs=spec,
        out_shape=jax.ShapeDtypeStruct(flat.shape, F32),
        compiler_params=_params(1),
        name="cache_roll",
    )(flat)
    return out.reshape(cache_t.shape)


Q_PAD = 16


def _attn_sample_body(q_ref, kn_ref, vn_ref, c_ref, tail_ref, o_ref, lse_ref, tail_out, *,
                      wb, window, dil, t):
    heads = range(N_HEADS)
    tail_out[...] = tail_ref[...]
    tq = lax.broadcasted_iota(jnp.int32, (Q_PAD, wb), 0)
    w = lax.broadcasted_iota(jnp.int32, (Q_PAD, wb), 1)
    dist = wb + tq - w
    cache_ok = ((dist & (dil - 1)) == 0) & (dist <= window)
    tq = lax.broadcasted_iota(jnp.int32, (Q_PAD, LANE), 0)
    n = lax.broadcasted_iota(jnp.int32, (Q_PAD, LANE), 1)
    dist = tq - n
    new_ok = (n < t) & (dist >= 0) & ((dist & (dil - 1)) == 0)
    zrow = jnp.zeros((Q_PAD - t, HEAD_DIM), F32)
    zcol = jnp.zeros((HEAD_DIM, LANE - t), F32)
    q = [jnp.concatenate([q_ref[0, 0, h], zrow], axis=0).astype(BF16) for h in heads]
    kn = [jnp.concatenate([kn_ref[0, 0, h], zcol], axis=1).astype(BF16) for h in heads]
    vn = [jnp.concatenate([vn_ref[0, 0, h], zcol], axis=1).astype(BF16) for h in heads]
    s_c = [jnp.where(cache_ok, _dot(q[h], c_ref[0, 0, 0, h].astype(BF16)), NEG_INF) for h in heads]
    s_n = [jnp.where(new_ok, _dot(q[h], kn[h]), NEG_INF) for h in heads]
    mx = [jnp.maximum(jnp.max(s_c[h], axis=-1, keepdims=True),
                      jnp.max(s_n[h], axis=-1, keepdims=True)) for h in heads]
    p_c = [jnp.exp(s_c[h] - mx[h]) for h in heads]
    p_n = [jnp.exp(s_n[h] - mx[h]) for h in heads]
    den = [jnp.sum(p_c[h], axis=-1, keepdims=True) + jnp.sum(p_n[h], axis=-1, keepdims=True)
           for h in heads]
    acc = [_dot_nt(p_c[h].astype(BF16), c_ref[0, 0, 1, h].astype(BF16))
           + _dot_nt(p_n[h].astype(BF16), vn[h]) for h in heads]
    for h in heads:
        o_ref[0, h] = (acc[h] / den[h])[0:t]
        lse_ref[0, h] = jnp.broadcast_to((mx[h] + jnp.log(den[h]))[0:t], (t, HEAD_DIM))
        tail_out[0, 0, 0, h, :, LANE - t:LANE] = kn_ref[0, 0, h]
        tail_out[0, 0, 1, h, :, LANE - t:LANE] = vn_ref[0, 0, h]


def _attn_sample(q_hm, kv_t, cache_t, stacked, layer, b, t, g):
    window, dil = ATT_GROUPS[g]
    wb = cache_t.shape[-1]
    assert wb == window and wb % LANE == 0 and t <= Q_PAD
    mat = (N_HEADS, HEAD_DIM)
    qspec = pl.BlockSpec((1, 1, N_HEADS, t, HEAD_DIM), lambda i: (i, g, 0, 0, 0))
    nspec = lambda part: pl.BlockSpec((1, 1) + mat + (t,), lambda i: (i, part * N_GROUPS + g, 0, 0, 0))
    cspec = pl.BlockSpec((1, 1, 2) + mat + (wb,), lambda i: (layer, i, 0, 0, 0, 0))
    tspec = pl.BlockSpec((1, 1, 2) + mat + (LANE,), lambda i: (layer, i, 0, 0, 0, wb // LANE - 1))
    ospec = pl.BlockSpec((1, N_HEADS, t, HEAD_DIM), lambda i: (i, 0, 0, 0))
    oshape = jax.ShapeDtypeStruct((b, N_HEADS, t, HEAD_DIM), F32)
    return pl.pallas_call(
        functools.partial(_attn_sample_body, wb=wb, window=window, dil=dil, t=t),
        grid=(b,),
        in_specs=[qspec, nspec(1), nspec(2), cspec, tspec],
        out_specs=[ospec, ospec, tspec],
        out_shape=[oshape, oshape, jax.ShapeDtypeStruct(cache_t.shape, F32)],
        input_output_aliases={4: 2},
        compiler_params=_params(1),
        name=f"attn_sample_g{g}",
    )(q_hm, kv_t, kv_t, cache_t, stacked)


def _merge_body(x_ref, pg_ref, y_ref, r_ref, k_ref, v_ref, g_ref, o0_ref, o1_ref, o2_ref,
                l0_ref, l1_ref, l2_ref, wa_ref, wb_ref, wo_ref, gng_ref, gnb_ref, rk_ref,
                lng_ref, lnb_ref, ones_ref, h_ref, *, alpha):
    ones_bd = ones_ref[...]
    y = y_ref[...]
    v = v_ref[...]
    mu = _head_sum(y, ones_bd) * (1.0 / HEAD_DIM)
    d = y - mu
    var = _head_sum(d * d, ones_bd) * (1.0 / HEAD_DIM)
    yn = d * lax.rsqrt(var + GN_EPS) * gng_ref[...] + gnb_ref[...]
    bonus = _head_sum(r_ref[...] * k_ref[...] * rk_ref[...], ones_bd) * v
    rw = (yn + bonus) * g_ref[...]
    l0, l1, l2 = l0_ref[...], l1_ref[...], l2_ref[...]
    mx = jnp.maximum(jnp.maximum(l0, l1), l2)
    e0, e1, e2 = jnp.exp(l0 - mx), jnp.exp(l1 - mx), jnp.exp(l2 - mx)
    att = (e0 * o0_ref[...] + e1 * o1_ref[...] + e2 * o2_ref[...]) / (e0 + e1 + e2)
    pg = pg_ref[...].astype(F32)
    merged = (_sigmoid(pg[:, :D_MODEL]) * _dot(rw.astype(BF16), wa_ref[...])
              + _sigmoid(pg[:, D_MODEL:]) * _dot(att.astype(BF16), wb_ref[...]))
    pre = alpha * x_ref[...] + _dot(merged.astype(BF16), wo_ref[...])
    h_ref[...] = _layer_norm(pre, lng_ref[...], lnb_ref[...])


def _merge(x, p_gate, y, r, k, v, g, outs, lses, lw, tm, alpha):
    m = x.shape[0]
    row = lambda c: pl.BlockSpec((tm, c), lambda i: (i, 0))
    const = lambda a: pl.BlockSpec(a.shape, lambda i: (0,) * a.ndim)
    consts = [lw["w_br_a"], lw["w_br_b"], lw["w_out"], lw["gn_g"], lw["gn_b"], lw["rk"],
              lw["ln1_g"], lw["ln1_b"], lw["ones_bd"]]
    return pl.pallas_call(
        functools.partial(_merge_body, alpha=alpha),
        grid=(m // tm,),
        in_specs=[row(D_MODEL), row(C_GATE)] + [row(WIDTH)] * 11 + [const(a) for a in consts],
        out_specs=row(D_MODEL),
        out_shape=jax.ShapeDtypeStruct((m, D_MODEL), F32),
        compiler_params=_params(1),
        name="merge_ln1",
    )(x, p_gate, y, r, k, v, g, *outs, *lses, *consts)


def _gelu(x):
    return 0.5 * x * (1.0 + lax.erf(x * (2.0 ** -0.5)))


def _ffn_body(*refs, seq, tm, nf, carry_mode, alpha):
    if carry_mode:
        (h_ref, wu_ref, wd_ref, cw_ref, cb_ref, lng_ref, lnb_ref, buf_ref,
         y_ref, cnew_ref, hb_ref, act_ref, carry_ref) = refs
    else:
        (h_ref, wu_ref, wd_ref, cw_ref, cb_ref, lng_ref, lnb_ref, e1_ref, e2_ref,
         y_ref, u_ref, hb_ref, act_ref) = refs
    i = pl.program_id(0)
    tf = FF_CHUNK
    hb_ref[...] = h_ref[...].astype(BF16)
    hb = hb_ref[...]
    row = lax.broadcasted_iota(jnp.int32, (tm, tf), 0)
    if carry_mode:
        slot = i % 2

        @pl.when(i % (seq // tm) == 0)
        def _():
            carry_ref[1 - slot, 6:8, :] = buf_ref[0]

    for c in range(nf):
        cols = slice(c * tf, (c + 1) * tf)
        u = _dot(hb, wu_ref[:, cols])
        gate = _dot(hb, wu_ref[:, D_FF + c * tf:D_FF + (c + 1) * tf])
        r1 = pltpu.roll(u, 1, 0)
        r2 = pltpu.roll(u, 2, 0)
        if carry_mode:
            tail = carry_ref[1 - slot, :, cols]
            u1 = jnp.where(row == 0, tail[7:8, :], r1)
            u2 = jnp.where(row == 0, tail[6:7, :], jnp.where(row == 1, tail[7:8, :], r2))
            carry_ref[slot, :, cols] = u[tm - 8:tm, :]
            cnew_ref[0, :, cols] = u[tm - 2:tm, :]
        else:
            u1 = jnp.where(row % seq == 0, e1_ref[:, cols], r1)
            u2 = jnp.where(row % seq < 2, e2_ref[:, cols], r2)
            u_ref[:, cols] = u
        conv = (cb_ref[:, cols] + cw_ref[0:1, cols] * u2 + cw_ref[1:2, cols] * u1
                + cw_ref[2:3, cols] * u)
        act_ref[:, cols] = (_gelu(conv) * gate).astype(BF16)
    down = _dot(act_ref[...], wd_ref[...])
    y_ref[...] = _layer_norm(alpha * h_ref[...] + down, lng_ref[...], lnb_ref[...])


def _conv_ffn(h, conv_buf, lw, seq, tm, carry_mode, alpha):
    m = h.shape[0]
    b = m // seq
    nf = D_FF // FF_CHUNK
    const = lambda a: pl.BlockSpec(a.shape, lambda i: (0,) * a.ndim)
    consts = [lw["w_up"], lw["w_down"], lw["conv_w"], lw["conv_b"], lw["ln2_g"], lw["ln2_b"]]
    in_specs = [pl.BlockSpec((tm, D_MODEL), lambda i: (i, 0))] + [const(a) for a in consts]
    args = [h] + consts
    scratch = [pltpu.VMEM((tm, D_MODEL), BF16), pltpu.VMEM((tm, D_FF), BF16)]
    y_spec = pl.BlockSpec((tm, D_MODEL), lambda i: (i, 0))
    y_shape = jax.ShapeDtypeStruct((m, D_MODEL), F32)
    if carry_mode:
        tps = seq // tm
        in_specs.append(pl.BlockSpec((1, CONV_W - 1, D_FF), lambda i: (i // tps, 0, 0)))
        args.append(conv_buf)
        out_specs = [y_spec, pl.BlockSpec((1, CONV_W - 1, D_FF), lambda i: (i, 0, 0))]
        out_shape = [y_shape, jax.ShapeDtypeStruct((m // tm, CONV_W - 1, D_FF), F32)]
        scratch.append(pltpu.VMEM((2, 8, D_FF), F32))
    else:
        zeros = jnp.zeros((b, seq, D_FF), F32)
        e1 = zeros.at[:, 0].set(conv_buf[:, 1]).reshape(m, D_FF)
        e2 = zeros.at[:, 0].set(conv_buf[:, 0]).at[:, 1].set(conv_buf[:, 1]).reshape(m, D_FF)
        tile = pl.BlockSpec((tm, D_FF), lambda i: (i, 0))
        in_specs += [tile, tile]
        args += [e1, e2]
        out_specs = [y_spec, tile]
        out_shape = [y_shape, jax.ShapeDtypeStruct((m, D_FF), F32)]
    y, aux = pl.pallas_call(
        functools.partial(_ffn_body, seq=seq, tm=tm, nf=nf, carry_mode=carry_mode, alpha=alpha),
        grid=(m // tm,),
        in_specs=in_specs,
        out_specs=out_specs,
        out_shape=out_shape,
        scratch_shapes=scratch,
        compiler_params=_params(1),
        name="conv_ffn_ln2",
    )(*args)
    if carry_mode:
        return y, aux[seq // tm - 1::seq // tm]
    return y, aux.reshape(b, seq, D_FF)[:, seq - (CONV_W - 1):]


def _rope_tables(pos):
    half = HEAD_DIM // 2
    inv = ROPE_THETA ** (-jnp.arange(half, dtype=F32) / half)
    ang = pos.astype(F32)[:, None] * inv[None, :]
    cos, sin = jnp.cos(ang), jnp.sin(ang)
    cos = jnp.tile(jnp.concatenate([cos, cos], axis=-1), (1, N_HEADS))
    sin = jnp.tile(jnp.concatenate([-sin, sin], axis=-1), (1, N_HEADS))
    return cos, sin


def _pad_cols(a, n):
    return jnp.pad(a, ((0, 0), (0, n - a.shape[1])))


def _layer_weights(l, w):
    row = lambda a: a.reshape(1, -1)
    z = lambda r, c: jnp.zeros((r, c), F32)
    w_in = w["w_in"][l]
    if l == 0:
        vres, mu_v = z(D_MODEL, D_MV_LORA), z(1, D_MV_LORA)
        v0, v2 = z(1, WIDTH), z(D_MV_LORA, WIDTH)
    else:
        vres, mu_v = w["w_in_vres"][l - 1], row(w["mu_vres"][l - 1])
        v0, v2 = row(w["rw_v0"][l - 1]), w["rw_v2"][l - 1]
    w_rw = _pad_cols(jnp.concatenate([w_in[:, C_ATT + C_GATE:], vres], axis=1), C_RW_PAD)
    mu = _pad_cols(jnp.concatenate([row(w["mu_rw"][l]), mu_v], axis=1), C_RW_PAD)
    w2a2 = jnp.concatenate([
        jnp.concatenate([w["rw_w2"][l], z(D_DECAY_LORA, WIDTH)], axis=1),
        jnp.concatenate([z(D_AAA_LORA, WIDTH), w["rw_a2"][l]], axis=1)], axis=0)
    pad_rows = 256 - D_GATE_LORA - D_MV_LORA
    g2v2 = jnp.concatenate([
        jnp.concatenate([w["rw_g2"][l], z(D_GATE_LORA, WIDTH)], axis=1),
        jnp.concatenate([z(D_MV_LORA, WIDTH), v2], axis=1),
        z(pad_rows, 2 * WIDTH)], axis=0)
    head = jnp.arange(WIDTH) // HEAD_DIM
    return {
        "w_att": w_in[:, :C_ATT].astype(BF16),
        "w_gate": w_in[:, C_ATT:C_ATT + C_GATE].astype(BF16),
        "w_rw": w_rw.astype(BF16),
        "mu": mu, "w0": row(w["rw_w0"][l]), "a0": row(w["rw_a0"][l]), "v0": v0,
        "kk": row(w["rw_kk"][l]), "ka": row(w["rw_ka"][l]),
        "w2a2": w2a2.astype(BF16), "g2v2": g2v2.astype(BF16),
        "ones_bd": (head[:, None] == head[None, :]).astype(BF16),
        "is_l0": jnp.full((1, WIDTH), 1.0 if l == 0 else 0.0, F32),
        "gn_g": row(w["rw_gn_g"][l]), "gn_b": row(w["rw_gn_b"][l]), "rk": row(w["rw_rk"][l]),
        "w_br_a": w["w_br_a"][l].astype(BF16), "w_br_b": w["w_br_b"][l].astype(BF16),
        "w_out": w["w_out"][l].astype(BF16),
        "ln1_g": row(w["ln1_g"][l]), "ln1_b": row(w["ln1_b"][l]),
        "w_up": w["ffn_w_up"][l].astype(BF16), "w_down": w["ffn_w_down"][l].astype(BF16),
        "conv_w": w["ffn_conv_w"][l], "conv_b": row(w["ffn_conv_b"][l]),
        "ln2_g": row(w["ln2_g"][l]), "ln2_b": row(w["ln2_b"][l]),
    }


def _trunk_layer(x, b, t, x_prev, wkv0, caches, conv_buf, v_first, lw, cos, sin, alpha, layer=0,
                 stacked=None):
    m = b * t
    prompt = caches is None
    tm = min(m, 1024 if prompt else 256)
    act_dtype = BF16 if prompt else F32
    p_att = _mm(x, lw["w_att"], tm, C_ATT // 3, act_dtype)
    p_gate = _mm(x, lw["w_gate"], tm, C_GATE, act_dtype)
    p_rw = _mm(x, lw["w_rw"], tm, C_RW_PAD)
    bp = -(-b // 8) * 8
    prev0 = _mm(jnp.pad(x_prev, ((0, bp - b), (0, 0))), lw["w_rw"], bp, C_RW_PAD // 3)[:b]

    if not prompt:
        p_nat = _rope_sample(p_att, jnp.tile(cos, (b, 1)), jnp.tile(sin, (b, 1)))
        p_nat = p_nat.reshape(b, t, 3 * N_GROUPS, N_HEADS, HEAD_DIM)
        q_hm = p_nat.transpose(0, 2, 3, 1, 4)
        kv_t = p_nat.transpose(0, 2, 3, 4, 1)
        flat = lambda a: a.transpose(0, 2, 1, 3).reshape(m, WIDTH)
    outs, lses, wins = [], [], []
    for g, (window, dil) in enumerate(ATT_GROUPS):
        if prompt:
            o, lse, kwin = _attn_prompt(p_att, cos, sin, b, t, g)
            lo = (2 * N_GROUPS + g) * WIDTH
            vwin = p_att.reshape(b, t, C_ATT)[:, t - window:, lo:lo + WIDTH].astype(F32)
            win = jnp.stack([kwin, vwin], axis=1).reshape(b, 2, window, N_HEADS, HEAD_DIM)
        else:
            o, lse, win = _attn_sample(q_hm, kv_t, caches[g], stacked[g], layer, b, t, g)
            o, lse = flat(o), flat(lse)
        outs.append(o)
        lses.append(lse)
        wins.append(win)

    carry_mode = t >= 512
    tm_rw = 512 if carry_mode else m
    if carry_mode:
        prev_in = prev0.reshape(b, 1, C_RW_PAD)
    else:
        prev_in = jnp.repeat(prev0, t, axis=0)
    vf_in = jnp.zeros((m, WIDTH), F32) if v_first is None else v_first
    r, lgw, k, v, kk, a, g_out = _time_mix(p_rw, prev_in, vf_in, lw, t, tm_rw, carry_mode)
    if v_first is None:
        v_first = v

    tp = -(-t // WKV_CHUNK) * WKV_CHUNK
    tok = lambda a_: jnp.pad(a_.reshape(b, t, WIDTH), ((0, 0), (0, tp - t), (0, 0)))
    y, wkv_new = _wkv(tok(r), tok(lgw), tok(k), tok(v), tok(kk), tok(a), wkv0)
    y = y[:, :t].reshape(m, WIDTH)

    tm_mg = min(m, 256)
    h = _merge(x, p_gate, y, r, k, v, g_out, outs, lses, lw, tm_mg, alpha)
    y_out, conv_new = _conv_ffn(h, conv_buf, lw, t, 512 if carry_mode else m, carry_mode, alpha)
    shift = x.reshape(b, t, D_MODEL)[:, -1]
    return y_out, wins, wkv_new, shift, conv_new, v_first


def kernel(x_prompt, x_sample, cache_win128, cache_win512, cache_win2048, state_wkv, state_shift, state_ffn_conv, w_in, w_in_vres, mu_rw, mu_vres, rw_w0, rw_w2, rw_a0, rw_a2, rw_g2, rw_v0, rw_v2, rw_kk, rw_ka, rw_rk, rw_gn_g, rw_gn_b, w_br_a, w_br_b, w_out, ln1_g, ln1_b, ffn_w_up, ffn_conv_w, ffn_conv_b, ffn_w_down, ln2_g, ln2_b):
    w = dict(w_in=w_in, w_in_vres=w_in_vres, mu_rw=mu_rw, mu_vres=mu_vres, rw_w0=rw_w0, rw_w2=rw_w2,
             rw_a0=rw_a0, rw_a2=rw_a2, rw_g2=rw_g2, rw_v0=rw_v0, rw_v2=rw_v2, rw_kk=rw_kk,
             rw_ka=rw_ka, rw_rk=rw_rk, rw_gn_g=rw_gn_g, rw_gn_b=rw_gn_b, w_br_a=w_br_a,
             w_br_b=w_br_b, w_out=w_out, ln1_g=ln1_g, ln1_b=ln1_b, ffn_w_up=ffn_w_up,
             ffn_conv_w=ffn_conv_w, ffn_conv_b=ffn_conv_b, ffn_w_down=ffn_w_down,
             ln2_g=ln2_g, ln2_b=ln2_b)
    caches = tuple(c.transpose(0, 1, 2, 4, 5, 3) for c in (cache_win128, cache_win512, cache_win2048))
    depth = w_in.shape[0]
    alpha = ALPHA
    bp, sp, _ = x_prompt.shape
    bs, ts, _ = x_sample.shape
    cos_p, sin_p = _rope_tables(jnp.arange(sp, dtype=jnp.int32))
    cos_s, sin_s = _rope_tables(PAST_LEN + jnp.arange(ts, dtype=jnp.int32))
    xp = x_prompt.reshape(bp * sp, D_MODEL)
    xs = x_sample.reshape(bs * ts, D_MODEL)
    vf_p = vf_s = None
    win_p = [[] for _ in ATT_GROUPS]
    win_s = [_cache_roll(c, ts) for c in caches]
    wkv_p, wkv_s, sh_p, sh_s, cv_p, cv_s = [], [], [], [], [], []
    for l in range(depth):
        lw = _layer_weights(l, w)
        xp, nw, s_new, sh, cv, vf_p = _trunk_layer(
            xp, bp, sp, jnp.zeros((bp, D_MODEL), F32),
            jnp.zeros((bp, N_HEADS, HEAD_DIM, HEAD_DIM), F32), None,
            jnp.zeros((bp, CONV_W - 1, D_FF), F32), vf_p, lw, cos_p, sin_p, alpha)
        for g in range(N_GROUPS):
            win_p[g].append(nw[g])
        wkv_p.append(s_new)
        sh_p.append(sh)
        cv_p.append(cv)
        xs, win_s, s_new, sh, cv, vf_s = _trunk_layer(
            xs, bs, ts, state_shift[l], state_wkv[l], caches,
            state_ffn_conv[l], vf_s, lw, cos_s, sin_s, alpha, l, win_s)
        wkv_s.append(s_new)
        sh_s.append(sh)
        cv_s.append(cv)
    return (xp.reshape(bp, sp, D_MODEL), xs.reshape(bs, ts, D_MODEL),
            jnp.stack(win_p[0]), win_s[0].transpose(0, 1, 2, 5, 3, 4),
            jnp.stack(win_p[1]), win_s[1].transpose(0, 1, 2, 5, 3, 4),
            jnp.stack(win_p[2]), win_s[2].transpose(0, 1, 2, 5, 3, 4),
            jnp.stack(wkv_p), jnp.stack(wkv_s),
            jnp.stack(sh_p), jnp.stack(sh_s), jnp.stack(cv_p), jnp.stack(cv_s))
```

```python
import functools

import jax
import jax.numpy as jnp
from jax import lax
from jax.experimental import pallas as pl
from jax.experimental.pallas import tpu as pltpu

F32 = jnp.float32
BF16 = jnp.bfloat16

D_MODEL = 1024
HEAD_DIM = 64
N_HEADS = 8
WIDTH = N_HEADS * HEAD_DIM
ATT_GROUPS = ((128, 1), (512, 4), (2048, 16))
N_GROUPS = len(ATT_GROUPS)
BLK = 128
ROPE_THETA = 10000.0
D_DECAY_LORA = 64
D_AAA_LORA = 64
D_GATE_LORA = 160
D_MV_LORA = 32
D_FF = 2816
CONV_W = 3
LN_EPS = 1e-5
GN_EPS = 64e-5
DEPTH = 4
ALPHA = (2.0 * DEPTH) ** 0.25
PAST_LEN = 8192
C_ATT = 3 * N_GROUPS * WIDTH
C_GATE = 2 * D_MODEL
C_RW = 3 * WIDTH + D_DECAY_LORA + D_AAA_LORA + D_GATE_LORA
C_RW_PAD = 1920
LORA_WA = 3 * WIDTH
LORA_GV = LORA_WA + 128
FF_CHUNK = 256
WKV_CHUNK = 64
ATT_ROWS = 2048
ATT_GROUP = 8
LANE = 128
VMEM_LIMIT = 56 * 1024 * 1024
NEG_INF = float("-inf")


def _params(n_axes):
    return pltpu.CompilerParams(
        dimension_semantics=("arbitrary",) * n_axes, vmem_limit_bytes=VMEM_LIMIT)


def _dot(a, b):
    return jnp.dot(a, b, preferred_element_type=F32)


NN = ((1,), (0,))
NT = ((1,), (1,))
TN = ((0,), (0,))


def _dg(a, b, dims):
    return lax.dot_general(a, b, (dims, ((), ())), preferred_element_type=F32)


def _dot_nt(a, b):
    return _dg(a, b, NT)


def _split(x):
    hi = x.astype(BF16)
    return hi, (x - hi.astype(F32)).astype(BF16)


def _dot_x1(a, b, dims=NN):
    return _dg(a.astype(BF16), b.astype(BF16), dims)


def _head_sum(x, ones_bd):
    hi, lo = _split(x)
    return _dot(hi, ones_bd) + _dot(lo, ones_bd)


def _layer_norm(x, g, b):
    mu = jnp.mean(x, axis=-1, keepdims=True)
    d = x - mu
    var = jnp.mean(d * d, axis=-1, keepdims=True)
    return d * lax.rsqrt(var + LN_EPS) * g + b


def _sigmoid(x):
    return 1.0 / (1.0 + jnp.exp(-x))


def _rope(t, cos, sin):
    half = HEAD_DIM // 2
    outs = []
    for c in range(t.shape[-1] // LANE):
        tc = t[:, c * LANE:(c + 1) * LANE]
        lane = lax.broadcasted_iota(jnp.int32, tc.shape, 1)
        fwd = pltpu.roll(tc, LANE - half, 1)
        bwd = pltpu.roll(tc, half, 1)
        outs.append(jnp.where((lane & (HEAD_DIM - 1)) < half, fwd, bwd))
    partner = jnp.concatenate(outs, axis=-1)
    return t * cos + partner * sin


def _mm_body(x_ref, w_ref, o_ref, xb_ref):
    @pl.when(pl.program_id(1) == 0)
    def _():
        xb_ref[...] = x_ref[...].astype(BF16)

    o_ref[...] = _dot(xb_ref[...], w_ref[...]).astype(o_ref.dtype)


def _mm(x, w, tm, tn, out_dtype=F32):
    m, k = x.shape
    n = w.shape[1]
    return pl.pallas_call(
        _mm_body,
        grid=(m // tm, n // tn),
        in_specs=[pl.BlockSpec((tm, k), lambda i, j: (i, 0)),
                  pl.BlockSpec((k, tn), lambda i, j: (0, j))],
        out_specs=pl.BlockSpec((tm, tn), lambda i, j: (i, j)),
        out_shape=jax.ShapeDtypeStruct((m, n), out_dtype),
        scratch_shapes=[pltpu.VMEM((tm, k), BF16)],
        compiler_params=_params(2),
        name="proj_mm",
    )(x, w)


def _tm_body(*refs, seq, tm, carry_mode):
    if carry_mode:
        (p_ref, prev0_ref, mu_ref, w0_ref, a0_ref, v0_ref, kkp_ref, kap_ref, w2a2_ref, g2v2_ref,
         ones_ref, vf_ref, l0_ref,
         r_out, lw_out, k_out, v_out, kk_out, a_out, g_out, carry_ref) = refs
    else:
        (p_ref, prev0_ref, mu_ref, w0_ref, a0_ref, v0_ref, kkp_ref, kap_ref, w2a2_ref, g2v2_ref,
         ones_ref, vf_ref, l0_ref,
         r_out, lw_out, k_out, v_out, kk_out, a_out, g_out) = refs
    p = p_ref[...]
    row = lax.broadcasted_iota(jnp.int32, p.shape, 0)
    rolled = pltpu.roll(p, 1, 0)
    if carry_mode:
        slot = pl.program_id(0) % 2

        @pl.when(pl.program_id(0) % (seq // tm) == 0)
        def _():
            carry_ref[1 - slot, 0:1, :] = prev0_ref[0]

        p_prev = jnp.where(row == 0, carry_ref[1 - slot, 0:1, :], rolled)
        carry_ref[slot, 0:1, :] = p[tm - 1:tm, :]
    else:
        p_prev = jnp.where(row % seq == 0, prev0_ref[...], rolled)
    z = p + (p_prev - p) * mu_ref[...]
    r = z[:, :WIDTH]
    k = z[:, WIDTH:2 * WIDTH]
    v = z[:, 2 * WIDTH:3 * WIDTH]
    wa = z[:, LORA_WA:LORA_WA + 128]
    lane = lax.broadcasted_iota(jnp.int32, wa.shape, 1)
    wa = jnp.where(lane < D_DECAY_LORA, jnp.tanh(wa), wa)
    wa_out = _dot(wa.astype(BF16), w2a2_ref[...])
    gv = z[:, LORA_GV:LORA_GV + 256]
    lane = lax.broadcasted_iota(jnp.int32, gv.shape, 1)
    gv = jnp.where(lane < D_GATE_LORA, _sigmoid(gv), gv)
    gv_out = _dot(gv.astype(BF16), g2v2_ref[...])
    x = -(w0_ref[...] + wa_out[:, :WIDTH])
    softplus = jnp.maximum(x, 0.0) + jnp.log1p(jnp.exp(-jnp.abs(x)))
    lw_out[...] = -jnp.exp(-softplus - 0.5)
    a = _sigmoid(a0_ref[...] + wa_out[:, WIDTH:])
    g_out[...] = gv_out[:, :WIDTH]
    vf = jnp.where(l0_ref[...] > 0.5, v, vf_ref[...])
    v = v + (vf - v) * _sigmoid(v0_ref[...] + gv_out[:, WIDTH:])
    kk = k * kkp_ref[...]
    norm = jnp.sqrt(_head_sum(kk * kk, ones_ref[...]))
    kk_out[...] = kk / jnp.maximum(norm, 1e-12)
    k_out[...] = k * (1.0 + (a - 1.0) * kap_ref[...])
    r_out[...] = r
    v_out[...] = v
    a_out[...] = a


def _time_mix(p_rw, prev0, vfirst, lw, seq, tm, carry_mode):
    m = p_rw.shape[0]
    row = lambda c: pl.BlockSpec((tm, c), lambda i: (i, 0))
    const = lambda a: pl.BlockSpec(a.shape, lambda i: (0,) * a.ndim)
    if carry_mode:
        prev_spec = pl.BlockSpec((1, 1, C_RW_PAD), lambda i: (i // (seq // tm), 0, 0))
        scratch = [pltpu.VMEM((2, 8, C_RW_PAD), F32)]
    else:
        prev_spec = row(C_RW_PAD)
        scratch = []
    consts = [lw["mu"], lw["w0"], lw["a0"], lw["v0"], lw["kk"], lw["ka"], lw["w2a2"], lw["g2v2"],
              lw["ones_bd"]]
    out = jax.ShapeDtypeStruct((m, WIDTH), F32)
    return pl.pallas_call(
        functools.partial(_tm_body, seq=seq, tm=tm, carry_mode=carry_mode),
        grid=(m // tm,),
        in_specs=[row(C_RW_PAD), prev_spec] + [const(a) for a in consts]
        + [row(WIDTH), const(lw["is_l0"])],
        out_specs=[row(WIDTH)] * 7,
        out_shape=[out] * 7,
        scratch_shapes=scratch,
        compiler_params=_params(1),
        name="time_mix",
    )(p_rw, prev0, *consts, vfirst, lw["is_l0"])


def _wkv_body(r_ref, lw_ref, k_ref, v_ref, kk_ref, a_ref, s0_ref, y_ref, sout_ref, s_ref, *,
              nc, nb):
    c = pl.program_id(1)
    n = WKV_CHUNK

    @pl.when(c == 0)
    def _():
        s_ref[...] = s0_ref[...]

    row = lax.broadcasted_iota(jnp.int32, (n, n), 0)
    col = lax.broadcasted_iota(jnp.int32, (n, n), 1)
    incl = col <= row
    strict = col < row
    eye = (col == row).astype(F32)
    row2 = lax.broadcasted_iota(jnp.int32, (n, 2 * n), 0)
    col2 = lax.broadcasted_iota(jnp.int32, (n, 2 * n), 1)
    incl_cat = (col2 & (n - 1)) <= row2
    tri = incl.astype(BF16)
    where, ar, bk, vh, bk_end, g_end = [], [], [], [], [], []
    for bi in range(nb):
        lw = lw_ref[bi]
        lw_hi = lw.astype(BF16)
        lw_mid, lw_lo = _split(lw - lw_hi.astype(F32))
        cum = _dot(tri, lw_hi) + _dot(tri, lw_mid) + _dot(tri, lw_lo)
        cum_end = cum[n - 1:n, :]
        kk = kk_ref[bi]
        k = k_ref[bi]
        kka = kk * a_ref[bi]
        e_neg = jnp.exp(-cum)
        e_end = jnp.exp(cum_end - cum)
        a_t = -kk * jnp.exp(cum - lw)
        b_t = kka * e_neg
        k_t = k * e_neg
        r_t = r_ref[bi] * jnp.exp(cum)
        b_h = kka * e_end
        k_h = k * e_end
        decay_end = jnp.exp(cum_end)
        v = v_ref[bi]
        for h in range(N_HEADS):
            sl = slice(h * HEAD_DIM, (h + 1) * HEAD_DIM)
            where.append((bi, h, sl))
            ar.append(jnp.concatenate([a_t[:, sl], r_t[:, sl]], axis=0))
            bk.append(jnp.concatenate([b_t[:, sl], k_t[:, sl]], axis=0))
            bk_end.append(jnp.concatenate([b_h[:, sl], k_h[:, sl]], axis=0))
            vh.append(v[:, sl])
            g_end.append(decay_end[:, sl])
    heads = range(len(where))
    s = [s_ref[bi, h] for bi, h, _ in where]
    gram = [_dot_x1(ar[h], bk[h], NT) for h in heads]
    on_s = [_dot_x1(ar[h], s[h], NT) for h in heads]
    l_ab = [jnp.where(strict, g[:n, :n], 0.0) for g in gram]
    l_ak = [jnp.where(strict, g[:n, n:], 0.0) for g in gram]
    m_cat = [jnp.where(incl_cat, g[n:, :], 0.0) for g in gram]
    rhs = [on_s[h][:n] + _dot_x1(l_ak[h], vh[h]) for h in heads]
    acc = [eye + l for l in l_ab]
    pw = [_dot_x1(l, l) for l in l_ab]
    span = 2
    while span * 2 < n:
        both = [_dot_x1(jnp.concatenate([acc[h], pw[h]], axis=0), pw[h]) for h in heads]
        acc = [acc[h] + both[h][:n] for h in heads]
        pw = [both[h][n:] for h in heads]
        span *= 2
    acc = [acc[h] + _dot_x1(acc[h], pw[h]) for h in heads]
    u = [_dot_x1(acc[h], rhs[h]) for h in heads]
    uv = [jnp.concatenate([u[h], vh[h]], axis=0) for h in heads]
    for h, (bi, _, sl) in enumerate(where):
        y_ref[bi, :, sl] = on_s[h][n:] + _dot_x1(m_cat[h], uv[h])
    for h, (bi, hd, _) in enumerate(where):
        s_ref[bi, hd] = s[h] * g_end[h] + _dot_x1(uv[h], bk_end[h], TN)

    @pl.when(c == nc - 1)
    def _():
        sout_ref[...] = s_ref[...]


def _wkv(r, lw, k, v, kk, a, s0):
    b, t, _ = r.shape
    nc = t // WKV_CHUNK
    nb = 2 if b % 2 == 0 else 1
    tok = pl.BlockSpec((nb, WKV_CHUNK, WIDTH), lambda i, c: (i, c, 0))
    st = pl.BlockSpec((nb, N_HEADS, HEAD_DIM, HEAD_DIM), lambda i, c: (i, 0, 0, 0))
    return pl.pallas_call(
        functools.partial(_wkv_body, nc=nc, nb=nb),
        grid=(b // nb, nc),
        in_specs=[tok] * 6 + [st],
        out_specs=[tok, st],
        out_shape=[jax.ShapeDtypeStruct((b, t, WIDTH), F32),
                   jax.ShapeDtypeStruct((b, N_HEADS, HEAD_DIM, HEAD_DIM), F32)],
        scratch_shapes=[pltpu.VMEM((nb, N_HEADS, HEAD_DIM, HEAD_DIM), F32)],
        compiler_params=_params(2),
        name="wkv_chunked",
    )(r, lw, k, v, kk, a, s0)


def _attn_prompt_body(q_ref, k_ref, v_ref, cos_ref, sin_ref, o_ref, lse_ref, kwin_ref,
                      qs_ref, ks_ref, vs_ref, kc_ref, vc_ref, *, nblk, dil, window):
    j = pl.program_id(2)
    slot = j % 2
    cos = cos_ref[...]
    sin = sin_ref[...]
    qs_ref[...] = _rope(q_ref[0].astype(F32), cos, sin) * (HEAD_DIM ** -0.5)
    k_rot = _rope(k_ref[0].astype(F32), cos, sin)
    ks_ref[...] = k_rot
    vs_ref[...] = v_ref[0].astype(F32)

    @pl.when(j == nblk - 1)
    def _():
        kwin_ref[0] = k_rot[ATT_ROWS - window:, :]

    @pl.when(j == 0)
    def _():
        kc_ref[1] = jnp.zeros((dil, BLK, LANE), BF16)
        vc_ref[1] = jnp.zeros((dil, BLK, LANE), BF16)

    lane = lax.broadcasted_iota(jnp.int32, (BLK, LANE), 1)
    first = lane < HEAD_DIM
    qi = lax.broadcasted_iota(jnp.int32, (BLK, BLK), 0)
    kj = lax.broadcasted_iota(jnp.int32, (BLK, BLK), 1)
    prev_ok = kj >= qi
    prev_ok_first = (kj - jnp.where(j > 0, 0, BLK)) >= qi
    cur_ok = kj <= qi
    mcount = ATT_ROWS // (BLK * dil)
    blocks = [(r, m) for r in range(dil) for m in range(mcount)]
    kcur, vcur = {}, {}
    for g0 in range(0, len(blocks), ATT_GROUP):
        group = blocks[g0:g0 + ATT_GROUP]
        rows, q, kp, vp, kb, vb, ok = [], [], [], [], [], [], []
        for r, m in group:
            start = r + dil * BLK * m
            rw = pl.ds(start, BLK, stride=dil) if dil > 1 else pl.ds(start, BLK)
            qf = qs_ref[rw, :]
            kcur[r, m] = ks_ref[rw, :].astype(BF16)
            vcur[r, m] = vs_ref[rw, :].astype(BF16)
            k_before = kc_ref[1 - slot, r] if m == 0 else kcur[r, m - 1]
            v_before = vc_ref[1 - slot, r] if m == 0 else vcur[r, m - 1]
            for q_head in (jnp.where(first, qf, 0.0), jnp.where(first, 0.0, qf)):
                q.append(q_head.astype(BF16))
                kp.append(k_before)
                vp.append(v_before)
                kb.append(kcur[r, m])
                vb.append(vcur[r, m])
                ok.append(prev_ok_first if m == 0 else prev_ok)
            rows.append(rw)
        chains = range(len(q))
        s_prev = [jnp.where(ok[c], _dot_nt(q[c], kp[c]), NEG_INF) for c in chains]
        s_cur = [jnp.where(cur_ok, _dot_nt(q[c], kb[c]), NEG_INF) for c in chains]
        mx = [jnp.maximum(jnp.max(s_prev[c], axis=-1, keepdims=True),
                          jnp.max(s_cur[c], axis=-1, keepdims=True)) for c in chains]
        p_prev = [jnp.exp(s_prev[c] - mx[c]) for c in chains]
        p_cur = [jnp.exp(s_cur[c] - mx[c]) for c in chains]
        den = [jnp.sum(p_prev[c], axis=-1, keepdims=True)
               + jnp.sum(p_cur[c], axis=-1, keepdims=True) for c in chains]
        acc = [_dot(p_prev[c].astype(BF16), vp[c]) + _dot(p_cur[c].astype(BF16), vb[c])
               for c in chains]
        lse = [mx[c] + jnp.log(den[c]) for c in chains]
        for idx, (r, m) in enumerate(group):
            c0, c1 = 2 * idx, 2 * idx + 1
            o_ref[0, rows[idx], :] = jnp.where(first, acc[c0] / den[c0], acc[c1] / den[c1])
            lse_ref[0, rows[idx], :] = jnp.where(first, lse[c0], lse[c1])
            if m == mcount - 1:
                kc_ref[slot, r] = kcur[r, m]
                vc_ref[slot, r] = vcur[r, m]


def _attn_prompt(p_att, cos, sin, b, s, g):
    window, dil = ATT_GROUPS[g]
    assert window // dil == BLK and window <= ATT_ROWS and s % ATT_ROWS == 0
    nblk = s // ATT_ROWS
    pairs = WIDTH // LANE
    pv = p_att.reshape(b, s, C_ATT)
    blk = lambda part: pl.BlockSpec(
        (1, ATT_ROWS, LANE), lambda i, hp, j: (i, j, (part * N_GROUPS + g) * pairs + hp))
    tspec = pl.BlockSpec((ATT_ROWS, LANE), lambda i, hp, j: (j, 0))
    ospec = pl.BlockSpec((1, ATT_ROWS, LANE), lambda i, hp, j: (i, j, hp))
    o, lse, kwin = pl.pallas_call(
        functools.partial(_attn_prompt_body, nblk=nblk, dil=dil, window=window),
        grid=(b, pairs, nblk),
        in_specs=[blk(0), blk(1), blk(2), tspec, tspec],
        out_specs=[ospec, ospec, pl.BlockSpec((1, window, LANE), lambda i, hp, j: (i, 0, hp))],
        out_shape=[jax.ShapeDtypeStruct((b, s, WIDTH), F32),
                   jax.ShapeDtypeStruct((b, s, WIDTH), F32),
                   jax.ShapeDtypeStruct((b, window, WIDTH), F32)],
        scratch_shapes=[pltpu.VMEM((ATT_ROWS, LANE), F32)] * 3 + [
                        pltpu.VMEM((2, dil, BLK, LANE), BF16), pltpu.VMEM((2, dil, BLK, LANE), BF16)],
        compiler_params=_params(3),
        name=f"attn_prompt_g{g}",
    )(pv, pv, pv, cos[:, :LANE], sin[:, :LANE])
    return o.reshape(b * s, WIDTH), lse.reshape(b * s, WIDTH), kwin


def _rope_sample_body(p_ref, cos_ref, sin_ref, o_ref):
    cos = cos_ref[...]
    sin = sin_ref[...]
    for part in range(3 * N_GROUPS):
        sl = slice(part * WIDTH, (part + 1) * WIDTH)
        x = p_ref[:, sl]
        if part < N_GROUPS:
            o_ref[:, sl] = _rope(x, cos, sin) * (HEAD_DIM ** -0.5)
        elif part < 2 * N_GROUPS:
            o_ref[:, sl] = _rope(x, cos, sin)
        else:
            o_ref[:, sl] = x


def _rope_sample(p_att, cos, sin):
    m = p_att.shape[0]
    full = lambda a: pl.BlockSpec(a.shape, lambda i: (0, 0))
    return pl.pallas_call(
        _rope_sample_body,
        grid=(1,),
        in_specs=[full(p_att), full(cos), full(sin)],
        out_specs=full(p_att),
        out_shape=jax.ShapeDtypeStruct((m, C_ATT), F32),
        compiler_params=_params(1),
        name="rope_sample",
    )(p_att, cos, sin)


def _cache_roll_body(c_ref, o_ref, *, wb, t):
    for kv in range(2):
        for h in range(N_HEADS):
            x = c_ref[0, kv, h]
            o_ref[0, kv, h] = jnp.concatenate([x[:, t:], x[:, :t]], axis=1)


def _cache_roll(cache_t, t):
    depth, b = cache_t.shape[:2]
    mat = cache_t.shape[2:]
    wb = mat[-1]
    n = depth * b
    flat = cache_t.reshape((n,) + mat)
    spec = pl.BlockSpec((1,) + mat, lambda i: (i, 0, 0, 0, 0))
    out = pl.pallas_call(
        functools.partial(_cache_roll_body, wb=wb, t=t),
        grid=(n,),
        in_specs=[spec],
        out_specs=spec,
        out_shape=jax.ShapeDtypeStruct(flat.shape, F32),
        compiler_params=_params(1),
        name="cache_roll",
    )(flat)
    return out.reshape(cache_t.shape)


Q_PAD = 16


def _attn_sample_body(q_ref, kn_ref, vn_ref, c_ref, tail_ref, o_ref, lse_ref, tail_out, *,
                      wb, window, dil, t):
    heads = range(N_HEADS)
    tail_out[...] = tail_ref[...]
    tq = lax.broadcasted_iota(jnp.int32, (Q_PAD, wb), 0)
    w = lax.broadcasted_iota(jnp.int32, (Q_PAD, wb), 1)
    dist = wb + tq - w
    cache_ok = ((dist & (dil - 1)) == 0) & (dist <= window)
    tq = lax.broadcasted_iota(jnp.int32, (Q_PAD, LANE), 0)
    n = lax.broadcasted_iota(jnp.int32, (Q_PAD, LANE), 1)
    dist = tq - n
    new_ok = (n < t) & (dist >= 0) & ((dist & (dil - 1)) == 0)
    zrow = jnp.zeros((Q_PAD - t, HEAD_DIM), F32)
    zcol = jnp.zeros((HEAD_DIM, LANE - t), F32)
    q = [jnp.concatenate([q_ref[0, 0, h], zrow], axis=0).astype(BF16) for h in heads]
    kn = [jnp.concatenate([kn_ref[0, 0, h], zcol], axis=1).astype(BF16) for h in heads]
    vn = [jnp.concatenate([vn_ref[0, 0, h], zcol], axis=1).astype(BF16) for h in heads]
    s_c = [jnp.where(cache_ok, _dot(q[h], c_ref[0, 0, 0, h].astype(BF16)), NEG_INF) for h in heads]
    s_n = [jnp.where(new_ok, _dot(q[h], kn[h]), NEG_INF) for h in heads]
    mx = [jnp.maximum(jnp.max(s_c[h], axis=-1, keepdims=True),
                      jnp.max(s_n[h], axis=-1, keepdims=True)) for h in heads]
    p_c = [jnp.exp(s_c[h] - mx[h]) for h in heads]
    p_n = [jnp.exp(s_n[h] - mx[h]) for h in heads]
    den = [jnp.sum(p_c[h], axis=-1, keepdims=True) + jnp.sum(p_n[h], axis=-1, keepdims=True)
           for h in heads]
    acc = [_dot_nt(p_c[h].astype(BF16), c_ref[0, 0, 1, h].astype(BF16))
           + _dot_nt(p_n[h].astype(BF16), vn[h]) for h in heads]
    for h in heads:
        o_ref[0, h] = (acc[h] / den[h])[0:t]
        lse_ref[0, h] = jnp.broadcast_to((mx[h] + jnp.log(den[h]))[0:t], (t, HEAD_DIM))
        tail_out[0, 0, 0, h, :, LANE - t:LANE] = kn_ref[0, 0, h]
        tail_out[0, 0, 1, h, :, LANE - t:LANE] = vn_ref[0, 0, h]


def _attn_sample(q_hm, kv_t, cache_t, stacked, layer, b, t, g):
    window, dil = ATT_GROUPS[g]
    wb = cache_t.shape[-1]
    assert wb == window and wb % LANE == 0 and t <= Q_PAD
    mat = (N_HEADS, HEAD_DIM)
    qspec = pl.BlockSpec((1, 1, N_HEADS, t, HEAD_DIM), lambda i: (i, g, 0, 0, 0))
    nspec = lambda part: pl.BlockSpec((1, 1) + mat + (t,), lambda i: (i, part * N_GROUPS + g, 0, 0, 0))
    cspec = pl.BlockSpec((1, 1, 2) + mat + (wb,), lambda i: (layer, i, 0, 0, 0, 0))
    tspec = pl.BlockSpec((1, 1, 2) + mat + (LANE,), lambda i: (layer, i, 0, 0, 0, wb // LANE - 1))
    ospec = pl.BlockSpec((1, N_HEADS, t, HEAD_DIM), lambda i: (i, 0, 0, 0))
    oshape = jax.ShapeDtypeStruct((b, N_HEADS, t, HEAD_DIM), F32)
    return pl.pallas_call(
        functools.partial(_attn_sample_body, wb=wb, window=window, dil=dil, t=t),
        grid=(b,),
        in_specs=[qspec, nspec(1), nspec(2), cspec, tspec],
        out_specs=[ospec, ospec, tspec],
        out_shape=[oshape, oshape, jax.ShapeDtypeStruct(cache_t.shape, F32)],
        input_output_aliases={4: 2},
        compiler_params=_params(1),
        name=f"attn_sample_g{g}",
    )(q_hm, kv_t, kv_t, cache_t, stacked)


def _merge_body(x_ref, pg_ref, y_ref, r_ref, k_ref, v_ref, g_ref, o0_ref, o1_ref, o2_ref,
                l0_ref, l1_ref, l2_ref, wa_ref, wb_ref, wo_ref, gng_ref, gnb_ref, rk_ref,
                lng_ref, lnb_ref, ones_ref, h_ref, *, alpha):
    ones_bd = ones_ref[...]
    y = y_ref[...]
    v = v_ref[...]
    mu = _head_sum(y, ones_bd) * (1.0 / HEAD_DIM)
    d = y - mu
    var = _head_sum(d * d, ones_bd) * (1.0 / HEAD_DIM)
    yn = d * lax.rsqrt(var + GN_EPS) * gng_ref[...] + gnb_ref[...]
    bonus = _head_sum(r_ref[...] * k_ref[...] * rk_ref[...], ones_bd) * v
    rw = (yn + bonus) * g_ref[...]
    l0, l1, l2 = l0_ref[...], l1_ref[...], l2_ref[...]
    mx = jnp.maximum(jnp.maximum(l0, l1), l2)
    e0, e1, e2 = jnp.exp(l0 - mx), jnp.exp(l1 - mx), jnp.exp(l2 - mx)
    att = (e0 * o0_ref[...] + e1 * o1_ref[...] + e2 * o2_ref[...]) / (e0 + e1 + e2)
    pg = pg_ref[...].astype(F32)
    merged = (_sigmoid(pg[:, :D_MODEL]) * _dot(rw.astype(BF16), wa_ref[...])
              + _sigmoid(pg[:, D_MODEL:]) * _dot(att.astype(BF16), wb_ref[...]))
    pre = alpha * x_ref[...] + _dot(merged.astype(BF16), wo_ref[...])
    h_ref[...] = _layer_norm(pre, lng_ref[...], lnb_ref[...])


def _merge(x, p_gate, y, r, k, v, g, outs, lses, lw, tm, alpha):
    m = x.shape[0]
    row = lambda c: pl.BlockSpec((tm, c), lambda i: (i, 0))
    const = lambda a: pl.BlockSpec(a.shape, lambda i: (0,) * a.ndim)
    consts = [lw["w_br_a"], lw["w_br_b"], lw["w_out"], lw["gn_g"], lw["gn_b"], lw["rk"],
              lw["ln1_g"], lw["ln1_b"], lw["ones_bd"]]
    return pl.pallas_call(
        functools.partial(_merge_body, alpha=alpha),
        grid=(m // tm,),
        in_specs=[row(D_MODEL), row(C_GATE)] + [row(WIDTH)] * 11 + [const(a) for a in consts],
        out_specs=row(D_MODEL),
        out_shape=jax.ShapeDtypeStruct((m, D_MODEL), F32),
        compiler_params=_params(1),
        name="merge_ln1",
    )(x, p_gate, y, r, k, v, g, *outs, *lses, *consts)


def _gelu(x):
    return 0.5 * x * (1.0 + lax.erf(x * (2.0 ** -0.5)))


def _ffn_body(*refs, seq, tm, nf, carry_mode, alpha):
    if carry_mode:
        (h_ref, wu_ref, wd_ref, cw_ref, cb_ref, lng_ref, lnb_ref, buf_ref,
         y_ref, cnew_ref, hb_ref, act_ref, carry_ref) = refs
    else:
        (h_ref, wu_ref, wd_ref, cw_ref, cb_ref, lng_ref, lnb_ref, e1_ref, e2_ref,
         y_ref, u_ref, hb_ref, act_ref) = refs
    i = pl.program_id(0)
    tf = FF_CHUNK
    hb_ref[...] = h_ref[...].astype(BF16)
    hb = hb_ref[...]
    row = lax.broadcasted_iota(jnp.int32, (tm, tf), 0)
    if carry_mode:
        slot = i % 2

        @pl.when(i % (seq // tm) == 0)
        def _():
            carry_ref[1 - slot, 6:8, :] = buf_ref[0]

    for c in range(nf):
        cols = slice(c * tf, (c + 1) * tf)
        u = _dot(hb, wu_ref[:, cols])
        gate = _dot(hb, wu_ref[:, D_FF + c * tf:D_FF + (c + 1) * tf])
        r1 = pltpu.roll(u, 1, 0)
        r2 = pltpu.roll(u, 2, 0)
        if carry_mode:
            tail = carry_ref[1 - slot, :, cols]
            u1 = jnp.where(row == 0, tail[7:8, :], r1)
            u2 = jnp.where(row == 0, tail[6:7, :], jnp.where(row == 1, tail[7:8, :], r2))
            carry_ref[slot, :, cols] = u[tm - 8:tm, :]
            cnew_ref[0, :, cols] = u[tm - 2:tm, :]
        else:
            u1 = jnp.where(row % seq == 0, e1_ref[:, cols], r1)
            u2 = jnp.where(row % seq < 2, e2_ref[:, cols], r2)
            u_ref[:, cols] = u
        conv = (cb_ref[:, cols] + cw_ref[0:1, cols] * u2 + cw_ref[1:2, cols] * u1
                + cw_ref[2:3, cols] * u)
        act_ref[:, cols] = (_gelu(conv) * gate).astype(BF16)
    down = _dot(act_ref[...], wd_ref[...])
    y_ref[...] = _layer_norm(alpha * h_ref[...] + down, lng_ref[...], lnb_ref[...])


def _conv_ffn(h, conv_buf, lw, seq, tm, carry_mode, alpha):
    m = h.shape[0]
    b = m // seq
    nf = D_FF // FF_CHUNK
    const = lambda a: pl.BlockSpec(a.shape, lambda i: (0,) * a.ndim)
    consts = [lw["w_up"], lw["w_down"], lw["conv_w"], lw["conv_b"], lw["ln2_g"], lw["ln2_b"]]
    in_specs = [pl.BlockSpec((tm, D_MODEL), lambda i: (i, 0))] + [const(a) for a in consts]
    args = [h] + consts
    scratch = [pltpu.VMEM((tm, D_MODEL), BF16), pltpu.VMEM((tm, D_FF), BF16)]
    y_spec = pl.BlockSpec((tm, D_MODEL), lambda i: (i, 0))
    y_shape = jax.ShapeDtypeStruct((m, D_MODEL), F32)
    if carry_mode:
        tps = seq // tm
        in_specs.append(pl.BlockSpec((1, CONV_W - 1, D_FF), lambda i: (i // tps, 0, 0)))
        args.append(conv_buf)
        out_specs = [y_spec, pl.BlockSpec((1, CONV_W - 1, D_FF), lambda i: (i, 0, 0))]
        out_shape = [y_shape, jax.ShapeDtypeStruct((m // tm, CONV_W - 1, D_FF), F32)]
        scratch.append(pltpu.VMEM((2, 8, D_FF), F32))
    else:
        zeros = jnp.zeros((b, seq, D_FF), F32)
        e1 = zeros.at[:, 0].set(conv_buf[:, 1]).reshape(m, D_FF)
        e2 = zeros.at[:, 0].set(conv_buf[:, 0]).at[:, 1].set(conv_buf[:, 1]).reshape(m, D_FF)
        tile = pl.BlockSpec((tm, D_FF), lambda i: (i, 0))
        in_specs += [tile, tile]
        args += [e1, e2]
        out_specs = [y_spec, tile]
        out_shape = [y_shape, jax.ShapeDtypeStruct((m, D_FF), F32)]
    y, aux = pl.pallas_call(
        functools.partial(_ffn_body, seq=seq, tm=tm, nf=nf, carry_mode=carry_mode, alpha=alpha),
        grid=(m // tm,),
        in_specs=in_specs,
        out_specs=out_specs,
        out_shape=out_shape,
        scratch_shapes=scratch,
        compiler_params=_params(1),
        name="conv_ffn_ln2",
    )(*args)
    if carry_mode:
        return y, aux[seq // tm - 1::seq // tm]
    return y, aux.reshape(b, seq, D_FF)[:, seq - (CONV_W - 1):]


def _rope_tables(pos):
    half = HEAD_DIM // 2
    inv = ROPE_THETA ** (-jnp.arange(half, dtype=F32) / half)
    ang = pos.astype(F32)[:, None] * inv[None, :]
    cos, sin = jnp.cos(ang), jnp.sin(ang)
    cos = jnp.tile(jnp.concatenate([cos, cos], axis=-1), (1, N_HEADS))
    sin = jnp.tile(jnp.concatenate([-sin, sin], axis=-1), (1, N_HEADS))
    return cos, sin


def _pad_cols(a, n):
    return jnp.pad(a, ((0, 0), (0, n - a.shape[1])))


def _layer_weights(l, w):
    row = lambda a: a.reshape(1, -1)
    z = lambda r, c: jnp.zeros((r, c), F32)
    w_in = w["w_in"][l]
    if l == 0:
        vres, mu_v = z(D_MODEL, D_MV_LORA), z(1, D_MV_LORA)
        v0, v2 = z(1, WIDTH), z(D_MV_LORA, WIDTH)
    else:
        vres, mu_v = w["w_in_vres"][l - 1], row(w["mu_vres"][l - 1])
        v0, v2 = row(w["rw_v0"][l - 1]), w["rw_v2"][l - 1]
    w_rw = _pad_cols(jnp.concatenate([w_in[:, C_ATT + C_GATE:], vres], axis=1), C_RW_PAD)
    mu = _pad_cols(jnp.concatenate([row(w["mu_rw"][l]), mu_v], axis=1), C_RW_PAD)
    w2a2 = jnp.concatenate([
        jnp.concatenate([w["rw_w2"][l], z(D_DECAY_LORA, WIDTH)], axis=1),
        jnp.concatenate([z(D_AAA_LORA, WIDTH), w["rw_a2"][l]], axis=1)], axis=0)
    pad_rows = 256 - D_GATE_LORA - D_MV_LORA
    g2v2 = jnp.concatenate([
        jnp.concatenate([w["rw_g2"][l], z(D_GATE_LORA, WIDTH)], axis=1),
        jnp.concatenate([z(D_MV_LORA, WIDTH), v2], axis=1),
        z(pad_rows, 2 * WIDTH)], axis=0)
    head = jnp.arange(WIDTH) // HEAD_DIM
    return {
        "w_att": w_in[:, :C_ATT].astype(BF16),
        "w_gate": w_in[:, C_ATT:C_ATT + C_GATE].astype(BF16),
        "w_rw": w_rw.astype(BF16),
        "mu": mu, "w0": row(w["rw_w0"][l]), "a0": row(w["rw_a0"][l]), "v0": v0,
        "kk": row(w["rw_kk"][l]), "ka": row(w["rw_ka"][l]),
        "w2a2": w2a2.astype(BF16), "g2v2": g2v2.astype(BF16),
        "ones_bd": (head[:, None] == head[None, :]).astype(BF16),
        "is_l0": jnp.full((1, WIDTH), 1.0 if l == 0 else 0.0, F32),
        "gn_g": row(w["rw_gn_g"][l]), "gn_b": row(w["rw_gn_b"][l]), "rk": row(w["rw_rk"][l]),
        "w_br_a": w["w_br_a"][l].astype(BF16), "w_br_b": w["w_br_b"][l].astype(BF16),
        "w_out": w["w_out"][l].astype(BF16),
        "ln1_g": row(w["ln1_g"][l]), "ln1_b": row(w["ln1_b"][l]),
        "w_up": w["ffn_w_up"][l].astype(BF16), "w_down": w["ffn_w_down"][l].astype(BF16),
        "conv_w": w["ffn_conv_w"][l], "conv_b": row(w["ffn_conv_b"][l]),
        "ln2_g": row(w["ln2_g"][l]), "ln2_b": row(w["ln2_b"][l]),
    }


def _trunk_layer(x, b, t, x_prev, wkv0, caches, conv_buf, v_first, lw, cos, sin, alpha, layer=0,
                 stacked=None):
    m = b * t
    prompt = caches is None
    tm = min(m, 1024 if prompt else 256)
    act_dtype = BF16 if prompt else F32
    p_att = _mm(x, lw["w_att"], tm, C_ATT // 3, act_dtype)
    p_gate = _mm(x, lw["w_gate"], tm, C_GATE, act_dtype)
    p_rw = _mm(x, lw["w_rw"], tm, C_RW_PAD)
    bp = -(-b // 8) * 8
    prev0 = _mm(jnp.pad(x_prev, ((0, bp - b), (0, 0))), lw["w_rw"], bp, C_RW_PAD // 3)[:b]

    if not prompt:
        p_nat = _rope_sample(p_att, jnp.tile(cos, (b, 1)), jnp.tile(sin, (b, 1)))
        p_nat = p_nat.reshape(b, t, 3 * N_GROUPS, N_HEADS, HEAD_DIM)
        q_hm = p_nat.transpose(0, 2, 3, 1, 4)
        kv_t = p_nat.transpose(0, 2, 3, 4, 1)
        flat = lambda a: a.transpose(0, 2, 1, 3).reshape(m, WIDTH)
    outs, lses, wins = [], [], []
    for g, (window, dil) in enumerate(ATT_GROUPS):
        if prompt:
            o, lse, kwin = _attn_prompt(p_att, cos, sin, b, t, g)
            lo = (2 * N_GROUPS + g) * WIDTH
            vwin = p_att.reshape(b, t, C_ATT)[:, t - window:, lo:lo + WIDTH].astype(F32)
            win = jnp.stack([kwin, vwin], axis=1).reshape(b, 2, window, N_HEADS, HEAD_DIM)
        else:
            o, lse, win = _attn_sample(q_hm, kv_t, caches[g], stacked[g], layer, b, t, g)
            o, lse = flat(o), flat(lse)
        outs.append(o)
        lses.append(lse)
        wins.append(win)

    carry_mode = t >= 512
    tm_rw = 512 if carry_mode else m
    if carry_mode:
        prev_in = prev0.reshape(b, 1, C_RW_PAD)
    else:
        prev_in = jnp.repeat(prev0, t, axis=0)
    vf_in = jnp.zeros((m, WIDTH), F32) if v_first is None else v_first
    r, lgw, k, v, kk, a, g_out = _time_mix(p_rw, prev_in, vf_in, lw, t, tm_rw, carry_mode)
    if v_first is None:
        v_first = v

    tp = -(-t // WKV_CHUNK) * WKV_CHUNK
    tok = lambda a_: jnp.pad(a_.reshape(b, t, WIDTH), ((0, 0), (0, tp - t), (0, 0)))
    y, wkv_new = _wkv(tok(r), tok(lgw), tok(k), tok(v), tok(kk), tok(a), wkv0)
    y = y[:, :t].reshape(m, WIDTH)

    tm_mg = min(m, 512)
    h = _merge(x, p_gate, y, r, k, v, g_out, outs, lses, lw, tm_mg, alpha)
    y_out, conv_new = _conv_ffn(h, conv_buf, lw, t, 512 if carry_mode else m, carry_mode, alpha)
    shift = x.reshape(b, t, D_MODEL)[:, -1]
    return y_out, wins, wkv_new, shift, conv_new, v_first


def kernel(x_prompt, x_sample, cache_win128, cache_win512, cache_win2048, state_wkv, state_shift, state_ffn_conv, w_in, w_in_vres, mu_rw, mu_vres, rw_w0, rw_w2, rw_a0, rw_a2, rw_g2, rw_v0, rw_v2, rw_kk, rw_ka, rw_rk, rw_gn_g, rw_gn_b, w_br_a, w_br_b, w_out, ln1_g, ln1_b, ffn_w_up, ffn_conv_w, ffn_conv_b, ffn_w_down, ln2_g, ln2_b):
    w = dict(w_in=w_in, w_in_vres=w_in_vres, mu_rw=mu_rw, mu_vres=mu_vres, rw_w0=rw_w0, rw_w2=rw_w2,
             rw_a0=rw_a0, rw_a2=rw_a2, rw_g2=rw_g2, rw_v0=rw_v0, rw_v2=rw_v2, rw_kk=rw_kk,
             rw_ka=rw_ka, rw_rk=rw_rk, rw_gn_g=rw_gn_g, rw_gn_b=rw_gn_b, w_br_a=w_br_a,
             w_br_b=w_br_b, w_out=w_out, ln1_g=ln1_g, ln1_b=ln1_b, ffn_w_up=ffn_w_up,
             ffn_conv_w=ffn_conv_w, ffn_conv_b=ffn_conv_b, ffn_w_down=ffn_w_down,
             ln2_g=ln2_g, ln2_b=ln2_b)
    caches = tuple(c.transpose(0, 1, 2, 4, 5, 3) for c in (cache_win128, cache_win512, cache_win2048))
    depth = w_in.shape[0]
    alpha = ALPHA
    bp, sp, _ = x_prompt.shape
    bs, ts, _ = x_sample.shape
    cos_p, sin_p = _rope_tables(jnp.arange(sp, dtype=jnp.int32))
    cos_s, sin_s = _rope_tables(PAST_LEN + jnp.arange(ts, dtype=jnp.int32))
    xp = x_prompt.reshape(bp * sp, D_MODEL)
    xs = x_sample.reshape(bs * ts, D_MODEL)
    vf_p = vf_s = None
    win_p = [[] for _ in ATT_GROUPS]
    win_s = [_cache_roll(c, ts) for c in caches]
    wkv_p, wkv_s, sh_p, sh_s, cv_p, cv_s = [], [], [], [], [], []
    for l in range(depth):
        lw = _layer_weights(l, w)
        xp, nw, s_new, sh, cv, vf_p = _trunk_layer(
            xp, bp, sp, jnp.zeros((bp, D_MODEL), F32),
            jnp.zeros((bp, N_HEADS, HEAD_DIM, HEAD_DIM), F32), None,
            jnp.zeros((bp, CONV_W - 1, D_FF), F32), vf_p, lw, cos_p, sin_p, alpha)
        for g in range(N_GROUPS):
            win_p[g].append(nw[g])
        wkv_p.append(s_new)
        sh_p.append(sh)
        cv_p.append(cv)
        xs, win_s, s_new, sh, cv, vf_s = _trunk_layer(
            xs, bs, ts, state_shift[l], state_wkv[l], caches,
            state_ffn_conv[l], vf_s, lw, cos_s, sin_s, alpha, l, win_s)
        wkv_s.append(s_new)
        sh_s.append(sh)
        cv_s.append(cv)
    return (xp.reshape(bp, sp, D_MODEL), xs.reshape(bs, ts, D_MODEL),
            jnp.stack(win_p[0]), win_s[0].transpose(0, 1, 2, 5, 3, 4),
            jnp.stack(win_p[1]), win_s[1].transpose(0, 1, 2, 5, 3, 4),
            jnp.stack(win_p[2]), win_s[2].transpose(0, 1, 2, 5, 3, 4),
            jnp.stack(wkv_p), jnp.stack(wkv_s),
            jnp.stack(sh_p), jnp.stack(sh_s), jnp.stack(cv_p), jnp.stack(cv_s))
```
